```python
import math
import jax, jax.numpy as jnp
from jax import lax
import numpy as np

D_MODEL = 1024
BATCH = 2
SEQ = 8192
DEPTH = 2

HEAD_DIM = 64
ROT_DIM = HEAD_DIM // 4
ROPE_THETA = 500000.0
DIFF_HEADS = 4
DIFF_VDIM = 2 * HEAD_DIM
SB_HEADS = 8
MOBA_HEADS = 8
MOBA_BLOCK = 256
MOBA_TOPK = 3
MOBA_Q_CHUNK = 64
Q_BLOCK = 128
FFN_DIM = 2816
NORM_EPS = 1e-6
DIFF_QK_W = DIFF_HEADS * 2 * HEAD_DIM
DIFF_W = DIFF_HEADS * DIFF_VDIM
SB_W = SB_HEADS * HEAD_DIM
MOBA_W = MOBA_HEADS * HEAD_DIM
IN_SIZES = (DIFF_QK_W, DIFF_QK_W, DIFF_W, SB_W, SB_W, SB_W, MOBA_W, MOBA_W, MOBA_W, D_MODEL, D_MODEL, D_MODEL)
IN_W = 3 * (DIFF_QK_W + DIFF_QK_W // 2) + 3 * SB_W + 3 * MOBA_W - 3 * (DIFF_QK_W // 2) + 3 * D_MODEL - 3 * DIFF_QK_W + DIFF_QK_W * 2 + DIFF_W

kernel_name = "hybrid_diff_stickbreak_moba_macaron"


def rms_norm(x, gain):
    xf = x.astype(jnp.float32)
    y = xf * lax.rsqrt(jnp.mean(xf * xf, axis=-1, keepdims=True) + NORM_EPS)
    return (y * gain.astype(jnp.float32)).astype(x.dtype)


def swiglu(x, wg, wu, wd):
    return (jax.nn.silu(x @ wg) * (x @ wu)) @ wd


def rope_tables(seq_len):
    pos = jnp.arange(seq_len, dtype=jnp.float32)
    inv_freq = ROPE_THETA ** (-jnp.arange(0, ROT_DIM, 2, dtype=jnp.float32) / ROT_DIM)
    ang = pos[:, None] * inv_freq[None, :]
    return jnp.cos(ang), jnp.sin(ang)


def partial_rope(x, cos, sin):
    half = ROT_DIM // 2
    c = cos.astype(x.dtype)
    s = sin.astype(x.dtype)
    x1 = x[..., :half]
    x2 = x[..., half:ROT_DIM]
    return jnp.concatenate([x1 * c - x2 * s, x2 * c + x1 * s, x[..., ROT_DIM:]], axis=-1)


def split_heads(t, n_heads):
    b, s, _ = t.shape
    return t.reshape(b, s, n_heads, -1).transpose(0, 2, 1, 3)


def merge_heads(t):
    b, h, s, d = t.shape
    return t.transpose(0, 2, 1, 3).reshape(b, s, h * d)


def sweep_blocks(fn, n_blocks):
    out = lax.map(fn, jnp.arange(n_blocks))
    n, b, h, blk, d = out.shape
    return out.transpose(1, 2, 0, 3, 4).reshape(b, h, n * blk, d)


def diff_attention(q, k, v, lam, cos, sin):
    seq = q.shape[3]
    q = partial_rope(q, cos, sin)
    k = partial_rope(k, cos, sin)
    scale = HEAD_DIM ** -0.5
    kpos = jnp.arange(seq)
    lam = lam.astype(jnp.float32)

    def block(i):
        q0 = i * Q_BLOCK
        qb = lax.dynamic_slice_in_dim(q, q0, Q_BLOCK, axis=3)
        logits = jnp.einsum('bhmqd,bhmkd->bhmqk', qb, k).astype(jnp.float32) * scale
        qpos = q0 + jnp.arange(Q_BLOCK)
        causal = kpos[None, :] <= qpos[:, None]
        p = jax.nn.softmax(jnp.where(causal, logits, -jnp.inf), axis=-1)
        w = p[:, :, 0] - lam * p[:, :, 1]
        return jnp.einsum('bhqk,bhkd->bhqd', w.astype(v.dtype), v)

    return sweep_blocks(block, seq // Q_BLOCK)


def stick_breaking_attention(q, k, v):
    seq = q.shape[2]
    scale = HEAD_DIM ** -0.5
    kpos = jnp.arange(seq)

    def block(i):
        q0 = i * Q_BLOCK
        qb = lax.dynamic_slice_in_dim(q, q0, Q_BLOCK, axis=2)
        z = jnp.einsum('bhqd,bhkd->bhqk', qb, k).astype(jnp.float32) * scale
        qpos = q0 + jnp.arange(Q_BLOCK)
        strict = kpos[None, :] < qpos[:, None]
        log_keep = jnp.where(strict, jax.nn.log_sigmoid(-z), 0.0)
        shifted = jnp.concatenate([log_keep[..., 1:], jnp.zeros_like(log_keep[..., :1])], axis=-1)
        log_surv = lax.cumsum(shifted, axis=3, reverse=True)
        w = jnp.where(strict, jnp.exp(jax.nn.log_sigmoid(z) + log_surv), 0.0)
        return jnp.einsum('bhqk,bhkd->bhqd', w.astype(v.dtype), v)

    return sweep_blocks(block, seq // Q_BLOCK)


def moba_attention(q, k, v, cos, sin):
    b, h, seq, hd = q.shape
    q = partial_rope(q, cos, sin)
    k = partial_rope(k, cos, sin)
    nb = -(-seq // MOBA_BLOCK)
    pad = nb * MOBA_BLOCK - seq
    kp = jnp.pad(k, ((0, 0), (0, 0), (0, pad), (0, 0)))
    vp = jnp.pad(v, ((0, 0), (0, 0), (0, pad), (0, 0)))
    kblk = kp.reshape(b, h, nb, MOBA_BLOCK, hd)
    vblk = vp.reshape(b, h, nb, MOBA_BLOCK, hd)
    kmean = jnp.mean(kblk, axis=3)
    topk = min(MOBA_TOPK, nb)
    scale = hd ** -0.5
    blk_ids = jnp.arange(nb)
    gather = jax.vmap(jax.vmap(lambda a, idx: a[idx]))

    def chunk(i):
        q0 = i * MOBA_Q_CHUNK
        qb = lax.dynamic_slice_in_dim(q, q0, MOBA_Q_CHUNK, axis=2)
        qpos = q0 + jnp.arange(MOBA_Q_CHUNK)
        own = q0 // MOBA_BLOCK
        gate = jnp.einsum('bhqd,bhnd->bhqn', qb, kmean).astype(jnp.float32)
        gate = jnp.where(blk_ids < own, gate, -jnp.inf)
        _, idx = lax.top_k(gate, topk)
        ksel = gather(kblk, idx)
        vsel = gather(vblk, idx)
        past_ok = jnp.arange(topk) < own
        lp = jnp.einsum('bhqd,bhqrkd->bhqrk', qb, ksel).astype(jnp.float32) * scale
        lp = jnp.where(past_ok[:, None], lp, -jnp.inf).reshape(b, h, MOBA_Q_CHUNK, topk * MOBA_BLOCK)
        kown = lax.dynamic_slice_in_dim(kp, own * MOBA_BLOCK, MOBA_BLOCK, axis=2)
        vown = lax.dynamic_slice_in_dim(vp, own * MOBA_BLOCK, MOBA_BLOCK, axis=2)
        lo = jnp.einsum('bhqd,bhkd->bhqk', qb, kown).astype(jnp.float32) * scale
        own_pos = own * MOBA_BLOCK + jnp.arange(MOBA_BLOCK)
        lo = jnp.where(own_pos[None, :] <= qpos[:, None], lo, -jnp.inf)
        p = jax.nn.softmax(jnp.concatenate([lp, lo], axis=-1), axis=-1)
        pp = p[..., :topk * MOBA_BLOCK].reshape(b, h, MOBA_Q_CHUNK, topk, MOBA_BLOCK).astype(v.dtype)
        po = p[..., topk * MOBA_BLOCK:].astype(v.dtype)
        return (jnp.einsum('bhqrk,bhqrkd->bhqd', pp, vsel)
                + jnp.einsum('bhqk,bhkd->bhqd', po, vown))

    return sweep_blocks(chunk, seq // MOBA_Q_CHUNK)


def hybrid_mixer(h, w_in, w_diff_o, w_sb_o, w_moba_o, w_out,
                 lq1, lk1, lq2, lk2, diff_g, lambda_init, cos, sin):
    b, s, _ = h.shape
    proj = h @ w_in
    offsets = np.cumsum(IN_SIZES)[:-1].tolist()
    (dq, dk, dv, sq, sk, sv, mq, mk, mv, g_diff, g_sb, g_moba) = jnp.split(proj, offsets, axis=-1)

    dq = dq.reshape(b, s, DIFF_HEADS, 2, HEAD_DIM).transpose(0, 2, 3, 1, 4)
    dk = dk.reshape(b, s, DIFF_HEADS, 2, HEAD_DIM).transpose(0, 2, 3, 1, 4)
    lam = (jnp.exp(jnp.sum(lq1.astype(jnp.float32) * lk1.astype(jnp.float32)))
           - jnp.exp(jnp.sum(lq2.astype(jnp.float32) * lk2.astype(jnp.float32))) + lambda_init)
    a_out = diff_attention(dq, dk, split_heads(dv, DIFF_HEADS), lam, cos, sin)
    a_out = merge_heads(rms_norm(a_out, diff_g) * (1.0 - lambda_init))

    b_out = merge_heads(stick_breaking_attention(split_heads(sq, SB_HEADS),
                                                 split_heads(sk, SB_HEADS),
                                                 split_heads(sv, SB_HEADS)))
    c_out = merge_heads(moba_attention(split_heads(mq, MOBA_HEADS),
                                       split_heads(mk, MOBA_HEADS),
                                       split_heads(mv, MOBA_HEADS), cos, sin))

    merged = (jax.nn.sigmoid(g_diff) * (a_out @ w_diff_o)
              + jax.nn.sigmoid(g_sb) * (b_out @ w_sb_o)
              + jax.nn.sigmoid(g_moba) * (c_out @ w_moba_o))
    return merged @ w_out


def setup_inputs(seed: int = 0) -> dict:
    key = jax.random.key(seed)
    ks = jax.random.split(key, 24)
    f32 = jnp.float32

    def nrm(k, shape, fan_in):
        return jax.random.normal(k, shape, f32) * (fan_in ** -0.5)

    def gain(k, shape):
        return 1.0 + 0.02 * jax.random.normal(k, shape, f32)

    L, D = DEPTH, D_MODEL
    return {
        "x": jax.random.normal(ks[0], (BATCH, SEQ, D), f32),
        "w_in": nrm(ks[1], (L, D, IN_W), D),
        "w_diff_o": nrm(ks[2], (L, DIFF_W, D), DIFF_W),
        "w_sb_o": nrm(ks[3], (L, SB_W, D), SB_W),
        "w_moba_o": nrm(ks[4], (L, MOBA_W, D), MOBA_W),
        "w_out": nrm(ks[5], (L, D, D), D),
        "lam_q1": 0.1 * jax.random.normal(ks[6], (L, HEAD_DIM), f32),
        "lam_k1": 0.1 * jax.random.normal(ks[7], (L, HEAD_DIM), f32),
        "lam_q2": 0.1 * jax.random.normal(ks[8], (L, HEAD_DIM), f32),
        "lam_k2": 0.1 * jax.random.normal(ks[9], (L, HEAD_DIM), f32),
        "diff_norm_g": gain(ks[10], (L, DIFF_VDIM)),
        "ffn1_wg": nrm(ks[11], (L, D, FFN_DIM), D),
        "ffn1_wu": nrm(ks[12], (L, D, FFN_DIM), D),
        "ffn1_wd": nrm(ks[13], (L, FFN_DIM, D), FFN_DIM),
        "ffn2_wg": nrm(ks[14], (L, D, FFN_DIM), D),
        "ffn2_wu": nrm(ks[15], (L, D, FFN_DIM), D),
        "ffn2_wd": nrm(ks[16], (L, FFN_DIM, D), FFN_DIM),
        "g_ffn1_pre": gain(ks[17], (L, D)),
        "g_ffn1_post": gain(ks[18], (L, D)),
        "g_mix_pre": gain(ks[19], (L, D)),
        "g_mix_post": gain(ks[20], (L, D)),
        "g_ffn2_pre": gain(ks[21], (L, D)),
        "g_ffn2_post": gain(ks[22], (L, D)),
    }


def reference(x, w_in, w_diff_o, w_sb_o, w_moba_o, w_out, lam_q1, lam_k1, lam_q2, lam_k2,
              diff_norm_g, ffn1_wg, ffn1_wu, ffn1_wd, ffn2_wg, ffn2_wu, ffn2_wd,
              g_ffn1_pre, g_ffn1_post, g_mix_pre, g_mix_post, g_ffn2_pre, g_ffn2_post):
    cos, sin = rope_tables(x.shape[1])
    for l in range(DEPTH):
        lambda_init = 0.8 - 0.6 * math.exp(-0.3 * l)
        x = x + 0.5 * rms_norm(swiglu(rms_norm(x, g_ffn1_pre[l]), ffn1_wg[l], ffn1_wu[l], ffn1_wd[l]),
                               g_ffn1_post[l])
        mixed = hybrid_mixer(rms_norm(x, g_mix_pre[l]), w_in[l], w_diff_o[l], w_sb_o[l], w_moba_o[l],
                             w_out[l], lam_q1[l], lam_k1[l], lam_q2[l], lam_k2[l], diff_norm_g[l],
                             lambda_init, cos, sin)
        x = x + rms_norm(mixed, g_mix_post[l])
        x = x + 0.5 * rms_norm(swiglu(rms_norm(x, g_ffn2_pre[l]), ffn2_wg[l], ffn2_wu[l], ffn2_wd[l]),
                               g_ffn2_post[l])
    return x
```

```python
import functools
import math

import jax
import jax.numpy as jnp
from jax import lax
from jax.experimental import pallas as pl
from jax.experimental.pallas import tpu as pltpu

F32 = jnp.float32
BF16 = jnp.bfloat16

LANES = 128
HEAD_DIM = 64
ROT_DIM = HEAD_DIM // 4
ROPE_THETA = 500000.0
MOBA_BLOCK = 256
MOBA_TOPK = 3
NORM_EPS = 1e-6
GROUP_W = 512
N_QKV_GROUPS = 9
N_GATE_GROUPS = 6
SB_LOG_ZERO = -104.0
VMEM_LIMIT = 56 * 1024 * 1024

_NT = (((1,), (1,)), ((), ()))


def _dot(a, b):
    return jnp.dot(a, b, preferred_element_type=F32)


def _dot_nt(a, b):
    return lax.dot_general(a, b, _NT, preferred_element_type=F32)


def _sigmoid(x):
    return 1.0 / (1.0 + jnp.exp(-x))


def _rms(x, gain):
    return x * lax.rsqrt(jnp.mean(x * x, axis=-1, keepdims=True) + NORM_EPS) * gain


def _params(n_axes):
    return pltpu.CompilerParams(dimension_semantics=("arbitrary",) * n_axes,
                                vmem_limit_bytes=VMEM_LIMIT)


def _ffn_body(x_ref, gpre_ref, wg_ref, wu_ref, wd_ref, gpost_ref, o_ref, xn_ref, acc_ref):
    f = pl.program_id(1)

    @pl.when(f == 0)
    def _():
        xn_ref[...] = _rms(x_ref[...], gpre_ref[...]).astype(BF16)
        acc_ref[...] = jnp.zeros_like(acc_ref)

    xn = xn_ref[...]
    g = _dot(xn, wg_ref[...])
    u = _dot(xn, wu_ref[...])
    a = (g * _sigmoid(g)) * u
    acc_ref[...] += _dot(a.astype(BF16), wd_ref[...])

    @pl.when(f == pl.num_programs(1) - 1)
    def _():
        o_ref[...] = x_ref[...] + 0.5 * _rms(acc_ref[...], gpost_ref[...])


def _ffn(x, gpre, wg, wu, wd, gpost, *, tm=512, tf=1408):
    t, d = x.shape
    f = wg.shape[1]
    return pl.pallas_call(
        _ffn_body,
        grid=(t // tm, f // tf),
        in_specs=[
            pl.BlockSpec((tm, d), lambda i, j: (i, 0)),
            pl.BlockSpec((1, d), lambda i, j: (0, 0)),
            pl.BlockSpec((d, tf), lambda i, j: (0, j)),
            pl.BlockSpec((d, tf), lambda i, j: (0, j)),
            pl.BlockSpec((tf, d), lambda i, j: (j, 0)),
            pl.BlockSpec((1, d), lambda i, j: (0, 0)),
        ],
        out_specs=pl.BlockSpec((tm, d), lambda i, j: (i, 0)),
        out_shape=jax.ShapeDtypeStruct((t, d), F32),
        scratch_shapes=[pltpu.VMEM((tm, d), BF16), pltpu.VMEM((tm, d), F32)],
        compiler_params=_params(2),
        name="ffn",
    )(x, gpre, wg, wu, wd, gpost)


def _inproj_body(x_ref, g_ref, w_ref, c_ref, sa_ref, sb_ref,
                 dq_ref, dk_ref, dv_ref, sq_ref, sk_ref, sv_ref, mq_ref, mk_ref, mv_ref,
                 gate_ref, kmean_ref):
    xn = _rms(x_ref[...], g_ref[...]).astype(BF16)
    cos, sin_a, sin_b = c_ref[...], sa_ref[...], sb_ref[...]
    scale = HEAD_DIM ** -0.5

    def proj(group):
        return _dot(xn, w_ref[:, group * GROUP_W:(group + 1) * GROUP_W])

    def rope(t):
        half = ROT_DIM // 2
        return (t * cos + pltpu.roll(t, LANES - half, 1) * sin_a
                + pltpu.roll(t, half, 1) * sin_b)

    def emit(out_ref, group, roped, scaled):
        t = proj(group)
        slabs = []
        for s in range(GROUP_W // LANES):
            ts = t[:, s * LANES:(s + 1) * LANES]
            if roped:
                ts = rope(ts)
            slabs.append(ts)
            out_ref[:, s * LANES:(s + 1) * LANES] = (ts * scale if scaled else ts).astype(BF16)
        return slabs

    emit(dq_ref, 0, True, True)
    emit(dk_ref, 1, True, False)
    emit(dv_ref, 2, False, False)
    emit(sq_ref, 3, False, True)
    emit(sk_ref, 4, False, False)
    emit(sv_ref, 5, False, False)
    emit(mq_ref, 6, True, True)
    mk_slabs = emit(mk_ref, 7, True, False)
    emit(mv_ref, 8, False, False)
    for s, ts in enumerate(mk_slabs):
        kmean_ref[0, :, s * LANES:(s + 1) * LANES] = jnp.mean(ts, axis=0, keepdims=True)
    for gi in range(N_GATE_GROUPS):
        gate_ref[:, gi * GROUP_W:(gi + 1) * GROUP_W] = _sigmoid(proj(N_QKV_GROUPS + gi))


def _inproj(x, g, w, cos_t, sina_t, sinb_t, seq):
    t, d = x.shape
    tm = MOBA_BLOCK
    n_pos_tiles = seq // tm
    qkv_spec = pl.BlockSpec((tm, GROUP_W), lambda i: (i, 0))
    tab_spec = pl.BlockSpec((tm, LANES), lambda i: (i % n_pos_tiles, 0))
    qkv_shape = jax.ShapeDtypeStruct((t, GROUP_W), BF16)
    gate_w = N_GATE_GROUPS * GROUP_W
    return pl.pallas_call(
        _inproj_body,
        grid=(t // tm,),
        in_specs=[
            pl.BlockSpec((tm, d), lambda i: (i, 0)),
            pl.BlockSpec((1, d), lambda i: (0, 0)),
            pl.BlockSpec(w.shape, lambda i: (0, 0), pipeline_mode=pl.Buffered(1)),
            tab_spec, tab_spec, tab_spec,
        ],
        out_specs=[qkv_spec] * N_QKV_GROUPS + [
            pl.BlockSpec((tm, gate_w), lambda i: (i, 0)),
            pl.BlockSpec((1, 1, GROUP_W), lambda i: (i, 0, 0)),
        ],
        out_shape=[qkv_shape] * N_QKV_GROUPS + [
            jax.ShapeDtypeStruct((t, gate_w), F32),
            jax.ShapeDtypeStruct((t // tm, 1, GROUP_W), F32),
        ],
        compiler_params=_params(1),
        name="inproj",
    )(x, g, w, cos_t, sina_t, sinb_t)


def _stack_heads(q):
    lane = lax.broadcasted_iota(jnp.int32, q.shape, 1)
    zero = jnp.zeros_like(q)
    return jnp.concatenate([jnp.where(lane < HEAD_DIM, q, zero),
                            jnp.where(lane >= HEAD_DIM, q, zero)], axis=0)


def _unstack_heads(o, tq):
    lane = lax.broadcasted_iota(jnp.int32, (tq, LANES), 1)
    return jnp.where(lane < HEAD_DIM, o[:tq], o[tq:])


def _tile_positions(tq, tk, q0, k0):
    row = lax.broadcasted_iota(jnp.int32, (2 * tq, tk), 0)
    row = jnp.where(row >= tq, row - tq, row) + q0
    col = lax.broadcasted_iota(jnp.int32, (2 * tq, tk), 1) + k0
    return row, col


def _softmax_step(s, vb, m_ref, l_ref, acc_ref):
    m_prev = m_ref[...]
    m_new = jnp.maximum(m_prev, jnp.max(s, axis=-1, keepdims=True))
    alpha = jnp.exp(m_prev - m_new)
    p = jnp.exp(s - m_new)
    l_ref[...] = alpha * l_ref[...] + jnp.sum(p, axis=-1, keepdims=True)
    acc_ref[...] = alpha * acc_ref[...] + _dot(p.astype(BF16), vb)
    m_ref[...] = m_new


def _softmax_init(m_ref, l_ref, acc_ref):
    m_ref[...] = jnp.full_like(m_ref, -jnp.inf)
    l_ref[...] = jnp.zeros_like(l_ref)
    acc_ref[...] = jnp.zeros_like(acc_ref)


def _attn_call(body, name, q, k, v, extra_inputs, extra_specs, scratch, *, batch, seq, tq):
    n_groups = GROUP_W // LANES
    rows_per_batch = seq // tq
    q_spec = pl.BlockSpec((tq, LANES), lambda b, h, i: (b * rows_per_batch + i, h))
    kv_spec = pl.BlockSpec((seq, LANES), lambda b, h, i: (b, h))
    return pl.pallas_call(
        body,
        grid=(batch, n_groups, rows_per_batch),
        in_specs=extra_specs + [q_spec, kv_spec, kv_spec],
        out_specs=q_spec,
        out_shape=jax.ShapeDtypeStruct(q.shape, BF16),
        scratch_shapes=scratch,
        compiler_params=_params(3),
        name=name,
    )(*extra_inputs, q, k, v)


def _diff_body(lam_ref, g_ref, q_ref, k_ref, v_ref, o_ref, m_ref, l_ref, acc_ref,
               *, tq, tk, post_scale):
    i = pl.program_id(2)
    qq = _stack_heads(q_ref[...])
    _softmax_init(m_ref, l_ref, acc_ref)

    def block(j, masked):
        k0 = pl.multiple_of(j * tk, tk)
        s = _dot_nt(qq, k_ref[pl.ds(k0, tk), :])
        if masked:
            row, col = _tile_positions(tq, tk, i * tq, k0)
            s = jnp.where(col <= row, s, -jnp.inf)
        _softmax_step(s, v_ref[pl.ds(k0, tk), :], m_ref, l_ref, acc_ref)

    n_full = (i * tq) // tk

    def full_block(j, carry):
        block(j, False)
        return carry

    lax.fori_loop(0, n_full, full_block, 0)
    block(n_full, True)

    o = acc_ref[...] / l_ref[...]
    out = o[:tq] - lam_ref[0, 0] * o[tq:]
    o_ref[...] = (_rms(out, g_ref[...]) * post_scale).astype(BF16)


def _diff_attention(q, k, v, lam, gain, *, batch, seq, post_scale, tq=256, tk=512):
    body = functools.partial(_diff_body, tq=tq, tk=tk, post_scale=post_scale)
    return _attn_call(
        body, "diff_attn", q, k, v,
        [lam, gain],
        [pl.BlockSpec(memory_space=pltpu.SMEM), pl.BlockSpec((1, LANES), lambda b, h, i: (0, 0))],
        [pltpu.VMEM((2 * tq, 1), F32), pltpu.VMEM((2 * tq, 1), F32), pltpu.VMEM((2 * tq, LANES), F32)],
        batch=batch, seq=seq, tq=tq)


def _sb_body(q_ref, k_ref, v_ref, o_ref, c_ref, acc_ref, *, tq, tk):
    i = pl.program_id(2)
    qq = _stack_heads(q_ref[...])
    c_ref[...] = jnp.zeros_like(c_ref)
    acc_ref[...] = jnp.zeros_like(acc_ref)
    later = (lax.broadcasted_iota(jnp.int32, (tk, tk), 0)
             > lax.broadcasted_iota(jnp.int32, (tk, tk), 1)).astype(BF16)

    def block(j, masked):
        k0 = pl.multiple_of(j * tk, tk)
        z = _dot_nt(qq, k_ref[pl.ds(k0, tk), :])
        softplus = jnp.maximum(z, 0.0) + jnp.log1p(jnp.exp(-jnp.abs(z)))
        log_keep = -softplus
        if masked:
            row, col = _tile_positions(tq, tk, i * tq, k0)
            valid = col < row
            log_keep = jnp.where(valid, log_keep, 0.0)
        hi = log_keep.astype(BF16)
        lo = (log_keep - hi.astype(F32)).astype(BF16)
        c = c_ref[...]
        log_w = (z - softplus) + (_dot(hi, later) + _dot(lo, later)) + c
        w = jnp.exp(log_w)
        if masked:
            w = jnp.where(valid, w, 0.0)
        acc_ref[...] += _dot(w.astype(BF16), v_ref[pl.ds(k0, tk), :])
        c_new = c + jnp.sum(log_keep, axis=-1, keepdims=True)
        c_ref[...] = c_new
        return jnp.max(c_new)

    def cond(state):
        j, c_max = state
        return jnp.logical_and(j >= 0, c_max > SB_LOG_ZERO)

    def body(state):
        j, _ = state
        return j - 1, block(j, False)

    lax.while_loop(cond, body, (i - 1, block(i, True)))
    o_ref[...] = _unstack_heads(acc_ref[...], tq).astype(BF16)


def _sb_attention(q, k, v, *, batch, seq, tq=256):
    body = functools.partial(_sb_body, tq=tq, tk=tq)
    return _attn_call(
        body, "sb_attn", q, k, v, [], [],
        [pltpu.VMEM((2 * tq, 1), F32), pltpu.VMEM((2 * tq, LANES), F32)],
        batch=batch, seq=seq, tq=tq)


def _moba_body(km_ref, q_ref, k_ref, v_ref, o_ref, m_ref, l_ref, acc_ref, bias_ref, *, tq):
    own = pl.program_id(2)
    tk = MOBA_BLOCK
    qq = _stack_heads(q_ref[...])

    gate = _dot_nt(qq, km_ref[0])
    lane = lax.broadcasted_iota(jnp.int32, gate.shape, 1)
    gate = jnp.where(lane < own, gate, -jnp.inf)
    selected = jnp.zeros(gate.shape, jnp.bool_)
    for _ in range(MOBA_TOPK):
        best = jnp.max(gate, axis=-1, keepdims=True)
        idx = jnp.min(jnp.where(gate == best, lane, LANES), axis=-1, keepdims=True)
        pick = jnp.logical_and(lane == idx, best > -jnp.inf)
        selected = jnp.logical_or(selected, pick)
        gate = jnp.where(lane == idx, -jnp.inf, gate)
    bias_ref[...] = jnp.where(selected, 0.0, -jnp.inf)

    _softmax_init(m_ref, l_ref, acc_ref)

    k0 = pl.multiple_of(own * tk, tk)
    s = _dot_nt(qq, k_ref[pl.ds(k0, tk), :])
    row, col = _tile_positions(tq, tk, 0, 0)
    s = jnp.where(col <= row, s, -jnp.inf)
    _softmax_step(s, v_ref[pl.ds(k0, tk), :], m_ref, l_ref, acc_ref)

    def past_block(j, carry):
        k0 = pl.multiple_of(j * tk, tk)
        s = _dot_nt(qq, k_ref[pl.ds(k0, tk), :])
        bias = jnp.sum(jnp.where(lane == j, bias_ref[...], 0.0), axis=-1, keepdims=True)
        _softmax_step(s + bias, v_ref[pl.ds(k0, tk), :], m_ref, l_ref, acc_ref)
        return carry

    lax.fori_loop(0, own, past_block, 0)
    o_ref[...] = _unstack_heads(acc_ref[...] / l_ref[...], tq).astype(BF16)


def _moba_attention(q, k, v, kmean, *, batch, seq):
    tq = MOBA_BLOCK
    body = functools.partial(_moba_body, tq=tq)
    return _attn_call(
        body, "moba_attn", q, k, v,
        [kmean],
        [pl.BlockSpec((1, LANES, LANES), lambda b, h, i: (b, 0, h))],
        [pltpu.VMEM((2 * tq, 1), F32), pltpu.VMEM((2 * tq, 1), F32),
         pltpu.VMEM((2 * tq, LANES), F32), pltpu.VMEM((2 * tq, LANES), F32)],
        batch=batch, seq=seq, tq=tq)


def _merge_body(x_ref, a_ref, b_ref, c_ref, gate_ref, wa_ref, wb_ref, wc_ref, wo_ref, g_ref, o_ref):
    d = x_ref.shape[1]
    merged = (gate_ref[:, 0:d] * _dot(a_ref[...], wa_ref[...])
              + gate_ref[:, d:2 * d] * _dot(b_ref[...], wb_ref[...])
              + gate_ref[:, 2 * d:3 * d] * _dot(c_ref[...], wc_ref[...]))
    y = _dot(merged.astype(BF16), wo_ref[...])
    o_ref[...] = x_ref[...] + _rms(y, g_ref[...])


def _merge(x, a, b, c, gates, wa, wb, wc, wo, g, *, tm=512):
    t, d = x.shape
    row = lambda w: pl.BlockSpec((tm, w), lambda i: (i, 0))
    whole = lambda arr: pl.BlockSpec(arr.shape, lambda i: (0, 0))
    return pl.pallas_call(
        _merge_body,
        grid=(t // tm,),
        in_specs=[row(d), row(GROUP_W), row(GROUP_W), row(GROUP_W), row(3 * d),
                  whole(wa), whole(wb), whole(wc), whole(wo), whole(g)],
        out_specs=row(d),
        out_shape=jax.ShapeDtypeStruct((t, d), F32),
        compiler_params=_params(1),
        name="merge",
    )(x, a, b, c, gates, wa, wb, wc, wo, g)


def _rope_lane_tables(seq):
    pos = jnp.arange(seq, dtype=F32)
    inv_freq = ROPE_THETA ** (-jnp.arange(0, ROT_DIM, 2, dtype=F32) / ROT_DIM)
    ang = pos[:, None] * inv_freq[None, :]
    cos, sin = jnp.cos(ang), jnp.sin(ang)
    half = ROT_DIM // 2
    rest = HEAD_DIM - ROT_DIM
    cos_h = jnp.concatenate([cos, cos, jnp.ones((seq, rest), F32)], axis=1)
    sina_h = jnp.concatenate([-sin, jnp.zeros((seq, half + rest), F32)], axis=1)
    sinb_h = jnp.concatenate([jnp.zeros((seq, half), F32), sin, jnp.zeros((seq, rest), F32)], axis=1)
    reps = LANES // HEAD_DIM
    return tuple(jnp.tile(t, (1, reps)) for t in (cos_h, sina_h, sinb_h))


def kernel(x, w_in, w_diff_o, w_sb_o, w_moba_o, w_out, lam_q1, lam_k1, lam_q2, lam_k2,
           diff_norm_g, ffn1_wg, ffn1_wu, ffn1_wd, ffn2_wg, ffn2_wu, ffn2_wd,
           g_ffn1_pre, g_ffn1_post, g_mix_pre, g_mix_post, g_ffn2_pre, g_ffn2_post):
    batch, seq, d = x.shape
    depth = w_in.shape[0]
    n_blocks = seq // MOBA_BLOCK
    assert seq % 512 == 0 and n_blocks <= LANES
    cos_t, sina_t, sinb_t = _rope_lane_tables(seq)
    bf = lambda w: w.astype(BF16)
    vec = lambda g: g.reshape(1, -1)
    xt = x.reshape(batch * seq, d)
    attn = dict(batch=batch, seq=seq)

    for l in range(depth):
        lambda_init = 0.8 - 0.6 * math.exp(-0.3 * l)
        xt = _ffn(xt, vec(g_ffn1_pre[l]), bf(ffn1_wg[l]), bf(ffn1_wu[l]), bf(ffn1_wd[l]),
                  vec(g_ffn1_post[l]))

        (dq, dk, dv, sq, sk, sv, mq, mk, mv, gates, kmean) = _inproj(
            xt, vec(g_mix_pre[l]), bf(w_in[l]), cos_t, sina_t, sinb_t, seq)
        lam = (jnp.exp(jnp.sum(lam_q1[l] * lam_k1[l])) - jnp.exp(jnp.sum(lam_q2[l] * lam_k2[l]))
               + lambda_init).reshape(1, 1)
        a_out = _diff_attention(dq, dk, dv, lam, vec(diff_norm_g[l]),
                                post_scale=1.0 - lambda_init, **attn)
        b_out = _sb_attention(sq, sk, sv, **attn)
        kmean = kmean.reshape(batch, n_blocks, GROUP_W)
        kmean = jnp.pad(kmean, ((0, 0), (0, LANES - n_blocks), (0, 0))).astype(BF16)
        c_out = _moba_attention(mq, mk, mv, kmean, **attn)

        xt = _merge(xt, a_out, b_out, c_out, gates, bf(w_diff_o[l]), bf(w_sb_o[l]), bf(w_moba_o[l]),
                    bf(w_out[l]), vec(g_mix_post[l]))

        xt = _ffn(xt, vec(g_ffn2_pre[l]), bf(ffn2_wg[l]), bf(ffn2_wu[l]), bf(ffn2_wd[l]),
                  vec(g_ffn2_post[l]))
    return xt.reshape(batch, seq, d)
```

```python
import functools
import math

import jax
import jax.numpy as jnp
from jax import lax
from jax.experimental import pallas as pl
from jax.experimental.pallas import tpu as pltpu

F32 = jnp.float32
BF16 = jnp.bfloat16

LANES = 128
HEAD_DIM = 64
ROT_DIM = HEAD_DIM // 4
ROPE_THETA = 500000.0
MOBA_BLOCK = 256
MOBA_TOPK = 3
NORM_EPS = 1e-6
GROUP_W = 512
N_GROUPS = GROUP_W // LANES
N_QKV_GROUPS = 9
N_GATE_GROUPS = 6
V_GROUPS = (2, 5, 8)
KEY_TILE = MOBA_BLOCK
SB_LOG_ZERO = -104.0
VMEM_LIMIT = 56 * 1024 * 1024

_NT = (((1,), (1,)), ((), ()))


def _dot(a, b):
    return jnp.dot(a, b, preferred_element_type=F32)


def _dot_nt(a, b):
    return lax.dot_general(a, b, _NT, preferred_element_type=F32)


def _sigmoid(x):
    return 1.0 / (1.0 + jnp.exp(-x))


def _rms(x, gain):
    return x * lax.rsqrt(jnp.mean(x * x, axis=-1, keepdims=True) + NORM_EPS) * gain


def _params(n_axes):
    return pltpu.CompilerParams(dimension_semantics=("arbitrary",) * n_axes,
                                vmem_limit_bytes=VMEM_LIMIT)


def _ffn_body(x_ref, gpre_ref, wg_ref, wu_ref, wd_ref, gpost_ref, o_ref, xn_ref, acc_ref):
    f = pl.program_id(1)

    @pl.when(f == 0)
    def _():
        xn_ref[...] = _rms(x_ref[...], gpre_ref[...]).astype(BF16)
        acc_ref[...] = jnp.zeros_like(acc_ref)

    xn = xn_ref[...]
    g = _dot(xn, wg_ref[...])
    u = _dot(xn, wu_ref[...])
    a = (g * _sigmoid(g)) * u
    acc_ref[...] += _dot(a.astype(BF16), wd_ref[...])

    @pl.when(f == pl.num_programs(1) - 1)
    def _():
        o_ref[...] = x_ref[...] + 0.5 * _rms(acc_ref[...], gpost_ref[...])


def _ffn(x, gpre, wg, wu, wd, gpost, *, tm=512, tf=1408):
    t, d = x.shape
    f = wg.shape[1]
    return pl.pallas_call(
        _ffn_body,
        grid=(t // tm, f // tf),
        in_specs=[
            pl.BlockSpec((tm, d), lambda i, j: (i, 0)),
            pl.BlockSpec((1, d), lambda i, j: (0, 0)),
            pl.BlockSpec((d, tf), lambda i, j: (0, j)),
            pl.BlockSpec((d, tf), lambda i, j: (0, j)),
            pl.BlockSpec((tf, d), lambda i, j: (j, 0)),
            pl.BlockSpec((1, d), lambda i, j: (0, 0)),
        ],
        out_specs=pl.BlockSpec((tm, d), lambda i, j: (i, 0)),
        out_shape=jax.ShapeDtypeStruct((t, d), F32),
        scratch_shapes=[pltpu.VMEM((tm, d), BF16), pltpu.VMEM((tm, d), F32)],
        compiler_params=_params(2),
        name="ffn",
    )(x, gpre, wg, wu, wd, gpost)


def _inproj_body(x_ref, g_ref, w_ref, wvt_ref, c_ref, sa_ref, sb_ref,
                 dq_ref, dk_ref, dvt_ref, sq_ref, sk_ref, svt_ref, mq_ref, mk_ref, mvt_ref,
                 gate_ref, kmean_ref):
    xn = _rms(x_ref[...], g_ref[...]).astype(BF16)
    cos, sin_a, sin_b = c_ref[...], sa_ref[...], sb_ref[...]
    scale = HEAD_DIM ** -0.5

    def proj(group):
        return _dot(xn, w_ref[:, group * GROUP_W:(group + 1) * GROUP_W])

    def rope(t):
        half = ROT_DIM // 2
        return (t * cos + pltpu.roll(t, LANES - half, 1) * sin_a
                + pltpu.roll(t, half, 1) * sin_b)

    def emit(out_ref, group, roped, scaled):
        t = proj(group)
        slabs = []
        for s in range(N_GROUPS):
            ts = t[:, s * LANES:(s + 1) * LANES]
            if roped:
                ts = rope(ts)
            slabs.append(ts)
            out_ref[:, s * LANES:(s + 1) * LANES] = (ts * scale if scaled else ts).astype(BF16)
        return slabs

    emit(dq_ref, 0, True, True)
    emit(dk_ref, 1, True, False)
    emit(sq_ref, 3, False, True)
    emit(sk_ref, 4, False, False)
    emit(mq_ref, 6, True, True)
    mk_slabs = emit(mk_ref, 7, True, False)
    for s, ts in enumerate(mk_slabs):
        kmean_ref[0, :, s * LANES:(s + 1) * LANES] = jnp.mean(ts, axis=0, keepdims=True)
    for n, vt_ref in enumerate((dvt_ref, svt_ref, mvt_ref)):
        vt_ref[0] = _dot_nt(wvt_ref[n], xn).astype(BF16)
    for gi in range(N_GATE_GROUPS):
        gate_ref[:, gi * GROUP_W:(gi + 1) * GROUP_W] = _sigmoid(proj(N_QKV_GROUPS + gi))


def _inproj(x, g, w, wvt, cos_t, sina_t, sinb_t, seq):
    t, d = x.shape
    tm = KEY_TILE
    n_pos_tiles = seq // tm
    qk_spec = pl.BlockSpec((tm, GROUP_W), lambda i: (i, 0))
    vt_spec = pl.BlockSpec((1, GROUP_W, tm), lambda i: (i, 0, 0))
    tab_spec = pl.BlockSpec((tm, LANES), lambda i: (i % n_pos_tiles, 0))
    qk_shape = jax.ShapeDtypeStruct((t, GROUP_W), BF16)
    vt_shape = jax.ShapeDtypeStruct((t // tm, GROUP_W, tm), BF16)
    gate_w = N_GATE_GROUPS * GROUP_W
    return pl.pallas_call(
        _inproj_body,
        grid=(t // tm,),
        in_specs=[
            pl.BlockSpec((tm, d), lambda i: (i, 0)),
            pl.BlockSpec((1, d), lambda i: (0, 0)),
            pl.BlockSpec(w.shape, lambda i: (0, 0), pipeline_mode=pl.Buffered(1)),
            pl.BlockSpec(wvt.shape, lambda i: (0, 0, 0), pipeline_mode=pl.Buffered(1)),
            tab_spec, tab_spec, tab_spec,
        ],
        out_specs=[qk_spec, qk_spec, vt_spec] * 3 + [
            pl.BlockSpec((tm, gate_w), lambda i: (i, 0)),
            pl.BlockSpec((1, 1, GROUP_W), lambda i: (i, 0, 0)),
        ],
        out_shape=[qk_shape, qk_shape, vt_shape] * 3 + [
            jax.ShapeDtypeStruct((t, gate_w), F32),
            jax.ShapeDtypeStruct((t // tm, 1, GROUP_W), F32),
        ],
        compiler_params=_params(1),
        name="inproj",
    )(x, g, w, wvt, cos_t, sina_t, sinb_t)


def _stack_heads(q):
    lane = lax.broadcasted_iota(jnp.int32, q.shape, 1)
    zero = jnp.zeros_like(q)
    return jnp.concatenate([jnp.where(lane < HEAD_DIM, q, zero),
                            jnp.where(lane >= HEAD_DIM, q, zero)], axis=0)


def _unstack_heads(o_t, tq):
    return jnp.concatenate([o_t[:HEAD_DIM, :tq], o_t[HEAD_DIM:, tq:]], axis=0).T


def _tile_positions(tk, tq, k0, q0):
    key = lax.broadcasted_iota(jnp.int32, (tk, 2 * tq), 0) + k0
    col = lax.broadcasted_iota(jnp.int32, (tk, 2 * tq), 1)
    query = jnp.where(col >= tq, col - tq, col) + q0
    return key, query


def _pv(vt_ref, j0, p):
    out = None
    for n in range(p.shape[0] // KEY_TILE):
        part = _dot(vt_ref[j0 + n], p[n * KEY_TILE:(n + 1) * KEY_TILE])
        out = part if out is None else out + part
    return out


def _softmax_step(s, vt_ref, j0, m_ref, l_ref, acc_ref):
    m_prev = m_ref[...]
    m_new = jnp.maximum(m_prev, jnp.max(s, axis=0, keepdims=True))
    alpha = jnp.exp(m_prev - m_new)
    p = jnp.exp(s - m_new)
    l_ref[...] = alpha * l_ref[...] + jnp.sum(p, axis=0, keepdims=True)
    acc_ref[...] = alpha * acc_ref[...] + _pv(vt_ref, j0, p.astype(BF16))
    m_ref[...] = m_new


def _softmax_init(m_ref, l_ref, acc_ref):
    m_ref[...] = jnp.full_like(m_ref, -jnp.inf)
    l_ref[...] = jnp.zeros_like(l_ref)
    acc_ref[...] = jnp.zeros_like(acc_ref)


def _attn_call(body, name, q, k, vt, extra_inputs, extra_specs, scratch, *, batch, seq, tq):
    rows_per_batch = seq // tq
    n_key_tiles = seq // KEY_TILE
    vt = vt.reshape(batch, n_key_tiles, GROUP_W, KEY_TILE)
    q_spec = pl.BlockSpec((tq, LANES), lambda b, h, i: (b * rows_per_batch + i, h))
    k_spec = pl.BlockSpec((seq, LANES), lambda b, h, i: (b, h))
    vt_spec = pl.BlockSpec((None, n_key_tiles, LANES, KEY_TILE), lambda b, h, i: (b, 0, h, 0))
    return pl.pallas_call(
        body,
        grid=(batch, N_GROUPS, rows_per_batch),
        in_specs=extra_specs + [q_spec, k_spec, vt_spec],
        out_specs=q_spec,
        out_shape=jax.ShapeDtypeStruct(q.shape, BF16),
        scratch_shapes=scratch,
        compiler_params=_params(3),
        name=name,
    )(*extra_inputs, q, k, vt)


def _row_stat(tq):
    return pltpu.VMEM((1, 2 * tq), F32)


def _acc(tq):
    return pltpu.VMEM((LANES, 2 * tq), F32)


def _diff_body(lam_ref, g_ref, q_ref, k_ref, vt_ref, o_ref, m_ref, l_ref, acc_ref,
               *, tq, tk, post_scale):
    i = pl.program_id(2)
    qq = _stack_heads(q_ref[...])
    _softmax_init(m_ref, l_ref, acc_ref)

    def block(j, masked):
        k0 = pl.multiple_of(j * tk, tk)
        s = _dot_nt(k_ref[pl.ds(k0, tk), :], qq)
        if masked:
            key, query = _tile_positions(tk, tq, k0, i * tq)
            s = jnp.where(key <= query, s, -jnp.inf)
        _softmax_step(s, vt_ref, j * (tk // KEY_TILE), m_ref, l_ref, acc_ref)

    n_full = (i * tq) // tk

    def full_block(j, carry):
        block(j, False)
        return carry

    lax.fori_loop(0, n_full, full_block, 0)
    block(n_full, True)

    o_t = acc_ref[...] / l_ref[...]
    out = (o_t[:, :tq] - lam_ref[0, 0] * o_t[:, tq:]).T
    o_ref[...] = (_rms(out, g_ref[...]) * post_scale).astype(BF16)


def _diff_attention(q, k, vt, lam, gain, *, batch, seq, post_scale, tq=256, tk=512):
    body = functools.partial(_diff_body, tq=tq, tk=tk, post_scale=post_scale)
    return _attn_call(
        body, "diff_attn", q, k, vt,
        [lam, gain],
        [pl.BlockSpec(memory_space=pltpu.SMEM), pl.BlockSpec((1, LANES), lambda b, h, i: (0, 0))],
        [_row_stat(tq), _row_stat(tq), _acc(tq)],
        batch=batch, seq=seq, tq=tq)


def _sb_body(q_ref, k_ref, vt_ref, o_ref, c_ref, acc_ref, *, tq):
    tk = KEY_TILE
    i = pl.program_id(2)
    qq = _stack_heads(q_ref[...])
    c_ref[...] = jnp.zeros_like(c_ref)
    acc_ref[...] = jnp.zeros_like(acc_ref)
    later = (lax.broadcasted_iota(jnp.int32, (tk, tk), 1)
             > lax.broadcasted_iota(jnp.int32, (tk, tk), 0)).astype(BF16)

    def block(j, masked):
        k0 = pl.multiple_of(j * tk, tk)
        z = _dot_nt(k_ref[pl.ds(k0, tk), :], qq)
        softplus = jnp.maximum(z, 0.0) + jnp.log1p(jnp.exp(-jnp.abs(z)))
        log_keep = -softplus
        if masked:
            key, query = _tile_positions(tk, tq, k0, i * tq)
            valid = key < query
            log_keep = jnp.where(valid, log_keep, 0.0)
        hi = log_keep.astype(BF16)
        lo = (log_keep - hi.astype(F32)).astype(BF16)
        c = c_ref[...]
        log_w = (z - softplus) + (_dot(later, hi) + _dot(later, lo)) + c
        w = jnp.exp(log_w)
        if masked:
            w = jnp.where(valid, w, 0.0)
        acc_ref[...] += _dot(vt_ref[j], w.astype(BF16))
        c_new = c + jnp.sum(log_keep, axis=0, keepdims=True)
        c_ref[...] = c_new
        return jnp.max(c_new)

    def cond(state):
        j, c_max = state
        return jnp.logical_and(j >= 0, c_max > SB_LOG_ZERO)

    def body(state):
        j, _ = state
        return j - 1, block(j, False)

    lax.while_loop(cond, body, (i - 1, block(i, True)))
    o_ref[...] = _unstack_heads(acc_ref[...], tq).astype(BF16)


def _sb_attention(q, k, vt, *, batch, seq):
    tq = KEY_TILE
    body = functools.partial(_sb_body, tq=tq)
    return _attn_call(body, "sb_attn", q, k, vt, [], [], [_row_stat(tq), _acc(tq)],
                      batch=batch, seq=seq, tq=tq)


def _moba_body(km_ref, q_ref, k_ref, vt_ref, o_ref, m_ref, l_ref, acc_ref, bias_ref, *, tq):
    own = pl.program_id(2)
    tk = MOBA_BLOCK
    qq = _stack_heads(q_ref[...])

    gate = _dot_nt(km_ref[0], qq)
    blk = lax.broadcasted_iota(jnp.int32, gate.shape, 0)
    gate = jnp.where(blk < own, gate, -jnp.inf)
    selected = jnp.zeros(gate.shape, jnp.bool_)
    for _ in range(MOBA_TOPK):
        best = jnp.max(gate, axis=0, keepdims=True)
        idx = jnp.min(jnp.where(gate == best, blk, LANES), axis=0, keepdims=True)
        pick = jnp.logical_and(blk == idx, best > -jnp.inf)
        selected = jnp.logical_or(selected, pick)
        gate = jnp.where(blk == idx, -jnp.inf, gate)
    bias_ref[...] = jnp.where(selected, 0.0, -jnp.inf)

    _softmax_init(m_ref, l_ref, acc_ref)

    k0 = pl.multiple_of(own * tk, tk)
    s = _dot_nt(k_ref[pl.ds(k0, tk), :], qq)
    key, query = _tile_positions(tk, tq, 0, 0)
    s = jnp.where(key <= query, s, -jnp.inf)
    _softmax_step(s, vt_ref, own, m_ref, l_ref, acc_ref)

    def past_block(j, carry):
        k0 = pl.multiple_of(j * tk, tk)
        s = _dot_nt(k_ref[pl.ds(k0, tk), :], qq)
        _softmax_step(s + bias_ref[pl.ds(j, 1), :], vt_ref, j, m_ref, l_ref, acc_ref)
        return carry

    lax.fori_loop(0, own, past_block, 0)
    o_ref[...] = _unstack_heads(acc_ref[...] / l_ref[...], tq).astype(BF16)


def _moba_attention(q, k, vt, kmean, *, batch, seq):
    tq = MOBA_BLOCK
    body = functools.partial(_moba_body, tq=tq)
    return _attn_call(
        body, "moba_attn", q, k, vt,
        [kmean],
        [pl.BlockSpec((1, LANES, LANES), lambda b, h, i: (b, 0, h))],
        [_row_stat(tq), _row_stat(tq), _acc(tq), _acc(tq)],
        batch=batch, seq=seq, tq=tq)


def _merge_body(x_ref, a_ref, b_ref, c_ref, gate_ref, wa_ref, wb_ref, wc_ref, wo_ref, g_ref, o_ref):
    d = x_ref.shape[1]
    merged = (gate_ref[:, 0:d] * _dot(a_ref[...], wa_ref[...])
              + gate_ref[:, d:2 * d] * _dot(b_ref[...], wb_ref[...])
              + gate_ref[:, 2 * d:3 * d] * _dot(c_ref[...], wc_ref[...]))
    y = _dot(merged.astype(BF16), wo_ref[...])
    o_ref[...] = x_ref[...] + _rms(y, g_ref[...])


def _merge(x, a, b, c, gates, wa, wb, wc, wo, g, *, tm=512):
    t, d = x.shape
    row = lambda w: pl.BlockSpec((tm, w), lambda i: (i, 0))
    whole = lambda arr: pl.BlockSpec(arr.shape, lambda i: (0, 0))
    return pl.pallas_call(
        _merge_body,
        grid=(t // tm,),
        in_specs=[row(d), row(GROUP_W), row(GROUP_W), row(GROUP_W), row(3 * d),
                  whole(wa), whole(wb), whole(wc), whole(wo), whole(g)],
        out_specs=row(d),
        out_shape=jax.ShapeDtypeStruct((t, d), F32),
        compiler_params=_params(1),
        name="merge",
    )(x, a, b, c, gates, wa, wb, wc, wo, g)


def _rope_lane_tables(seq):
    pos = jnp.arange(seq, dtype=F32)
    inv_freq = ROPE_THETA ** (-jnp.arange(0, ROT_DIM, 2, dtype=F32) / ROT_DIM)
    ang = pos[:, None] * inv_freq[None, :]
    cos, sin = jnp.cos(ang), jnp.sin(ang)
    half = ROT_DIM // 2
    rest = HEAD_DIM - ROT_DIM
    cos_h = jnp.concatenate([cos, cos, jnp.ones((seq, rest), F32)], axis=1)
    sina_h = jnp.concatenate([-sin, jnp.zeros((seq, half + rest), F32)], axis=1)
    sinb_h = jnp.concatenate([jnp.zeros((seq, half), F32), sin, jnp.zeros((seq, rest), F32)], axis=1)
    reps = LANES // HEAD_DIM
    return tuple(jnp.tile(t, (1, reps)) for t in (cos_h, sina_h, sinb_h))


def kernel(x, w_in, w_diff_o, w_sb_o, w_moba_o, w_out, lam_q1, lam_k1, lam_q2, lam_k2,
           diff_norm_g, ffn1_wg, ffn1_wu, ffn1_wd, ffn2_wg, ffn2_wu, ffn2_wd,
           g_ffn1_pre, g_ffn1_post, g_mix_pre, g_mix_post, g_ffn2_pre, g_ffn2_post):
    batch, seq, d = x.shape
    depth = w_in.shape[0]
    n_blocks = seq // MOBA_BLOCK
    assert seq % 512 == 0 and n_blocks <= LANES
    cos_t, sina_t, sinb_t = _rope_lane_tables(seq)
    bf = lambda w: w.astype(BF16)
    vec = lambda g: g.reshape(1, -1)
    xt = x.reshape(batch * seq, d)
    attn = dict(batch=batch, seq=seq)

    for l in range(depth):
        lambda_init = 0.8 - 0.6 * math.exp(-0.3 * l)
        xt = _ffn(xt, vec(g_ffn1_pre[l]), bf(ffn1_wg[l]), bf(ffn1_wu[l]), bf(ffn1_wd[l]),
                  vec(g_ffn1_post[l]))

        w_l = bf(w_in[l])
        w_vt = jnp.stack([w_l[:, g * GROUP_W:(g + 1) * GROUP_W].T for g in V_GROUPS])
        (dq, dk, dvt, sq, sk, svt, mq, mk, mvt, gates, kmean) = _inproj(
            xt, vec(g_mix_pre[l]), w_l, w_vt, cos_t, sina_t, sinb_t, seq)
        lam = (jnp.exp(jnp.sum(lam_q1[l] * lam_k1[l])) - jnp.exp(jnp.sum(lam_q2[l] * lam_k2[l]))
               + lambda_init).reshape(1, 1)
        a_out = _diff_attention(dq, dk, dvt, lam, vec(diff_norm_g[l]),
                                post_scale=1.0 - lambda_init, **attn)
        b_out = _sb_attention(sq, sk, svt, **attn)
        kmean = kmean.reshape(batch, n_blocks, GROUP_W)
        kmean = jnp.pad(kmean, ((0, 0), (0, LANES - n_blocks), (0, 0))).astype(BF16)
        c_out = _moba_attention(mq, mk, mvt, kmean, **attn)

        xt = _merge(xt, a_out, b_out, c_out, gates, bf(w_diff_o[l]), bf(w_sb_o[l]), bf(w_moba_o[l]),
                    bf(w_out[l]), vec(g_mix_post[l]))

        xt = _ffn(xt, vec(g_ffn2_pre[l]), bf(ffn2_wg[l]), bf(ffn2_wu[l]), bf(ffn2_wd[l]),
                  vec(g_ffn2_post[l]))
    return xt.reshape(batch, seq, d)
```

```python
import functools
import math

import jax
import jax.numpy as jnp
from jax import lax
from jax.experimental import pallas as pl
from jax.experimental.pallas import tpu as pltpu

F32 = jnp.float32
BF16 = jnp.bfloat16

LANES = 128
HEAD_DIM = 64
ROT_DIM = HEAD_DIM // 4
ROPE_THETA = 500000.0
MOBA_BLOCK = 256
MOBA_TOPK = 3
NORM_EPS = 1e-6
GROUP_W = 512
N_GROUPS = GROUP_W // LANES
N_QKV_GROUPS = 9
N_GATE_GROUPS = 6
V_GROUPS = (2, 5, 8)
KEY_TILE = MOBA_BLOCK
SB_LOG_ZERO = -104.0
MASKED = -1e30
SUM_ROWS = 16
ATTN_TQ = 512
VMEM_LIMIT = 56 * 1024 * 1024

_NT = (((1,), (1,)), ((), ()))


def _dot(a, b):
    return jnp.dot(a, b, preferred_element_type=F32)


def _dot_nt(a, b):
    return lax.dot_general(a, b, _NT, preferred_element_type=F32)


def _sigmoid(x):
    return 1.0 / (1.0 + jnp.exp(-x))


def _rms(x, gain):
    return x * lax.rsqrt(jnp.mean(x * x, axis=-1, keepdims=True) + NORM_EPS) * gain


def _params(n_axes):
    return pltpu.CompilerParams(dimension_semantics=("arbitrary",) * n_axes,
                                vmem_limit_bytes=VMEM_LIMIT)


def _ffn_body(x_ref, gpre_ref, wg_ref, wu_ref, wd_ref, gpost_ref, o_ref, xn_ref, acc_ref):
    f = pl.program_id(1)

    @pl.when(f == 0)
    def _():
        xn_ref[...] = _rms(x_ref[...], gpre_ref[...]).astype(BF16)
        acc_ref[...] = jnp.zeros_like(acc_ref)

    xn = xn_ref[...]
    g = _dot(xn, wg_ref[...])
    u = _dot(xn, wu_ref[...])
    a = (g * _sigmoid(g)) * u
    acc_ref[...] += _dot(a.astype(BF16), wd_ref[...])

    @pl.when(f == pl.num_programs(1) - 1)
    def _():
        o_ref[...] = x_ref[...] + 0.5 * _rms(acc_ref[...], gpost_ref[...])


def _ffn(x, gpre, wg, wu, wd, gpost, *, tm=512, tf=1408):
    t, d = x.shape
    f = wg.shape[1]
    return pl.pallas_call(
        _ffn_body,
        grid=(t // tm, f // tf),
        in_specs=[
            pl.BlockSpec((tm, d), lambda i, j: (i, 0)),
            pl.BlockSpec((1, d), lambda i, j: (0, 0)),
            pl.BlockSpec((d, tf), lambda i, j: (0, j)),
            pl.BlockSpec((d, tf), lambda i, j: (0, j)),
            pl.BlockSpec((tf, d), lambda i, j: (j, 0)),
            pl.BlockSpec((1, d), lambda i, j: (0, 0)),
        ],
        out_specs=pl.BlockSpec((tm, d), lambda i, j: (i, 0)),
        out_shape=jax.ShapeDtypeStruct((t, d), F32),
        scratch_shapes=[pltpu.VMEM((tm, d), BF16), pltpu.VMEM((tm, d), F32)],
        compiler_params=_params(2),
        name="ffn",
    )(x, gpre, wg, wu, wd, gpost)


def _inproj_body(x_ref, g_ref, w_ref, wvt_ref, c_ref, sa_ref, sb_ref,
                 dq_ref, dk_ref, dvt_ref, sq_ref, sk_ref, svt_ref, mq_ref, mk_ref, mvt_ref,
                 gate_ref, kmean_ref):
    xn = _rms(x_ref[...], g_ref[...]).astype(BF16)
    cos, sin_a, sin_b = c_ref[...], sa_ref[...], sb_ref[...]
    scale = HEAD_DIM ** -0.5

    def proj(group):
        return _dot(xn, w_ref[:, group * GROUP_W:(group + 1) * GROUP_W])

    def rope(t):
        half = ROT_DIM // 2
        return (t * cos + pltpu.roll(t, LANES - half, 1) * sin_a
                + pltpu.roll(t, half, 1) * sin_b)

    def emit(out_ref, group, roped, scaled):
        t = proj(group)
        slabs = []
        for s in range(N_GROUPS):
            ts = t[:, s * LANES:(s + 1) * LANES]
            if roped:
                ts = rope(ts)
            slabs.append(ts)
            out_ref[:, s * LANES:(s + 1) * LANES] = (ts * scale if scaled else ts).astype(BF16)
        return slabs

    emit(dq_ref, 0, True, True)
    emit(dk_ref, 1, True, False)
    emit(sq_ref, 3, False, True)
    emit(sk_ref, 4, False, False)
    emit(mq_ref, 6, True, True)
    mk_slabs = emit(mk_ref, 7, True, False)
    for s, ts in enumerate(mk_slabs):
        kmean_ref[0, :, s * LANES:(s + 1) * LANES] = jnp.mean(ts, axis=0, keepdims=True)
    for n, vt_ref in enumerate((dvt_ref, svt_ref, mvt_ref)):
        vt_ref[0] = _dot_nt(wvt_ref[n], xn).astype(BF16)
    for gi in range(N_GATE_GROUPS):
        gate_ref[:, gi * GROUP_W:(gi + 1) * GROUP_W] = _sigmoid(proj(N_QKV_GROUPS + gi))


def _inproj(x, g, w, wvt, cos_t, sina_t, sinb_t, seq):
    t, d = x.shape
    tm = KEY_TILE
    n_pos_tiles = seq // tm
    qk_spec = pl.BlockSpec((tm, GROUP_W), lambda i: (i, 0))
    vt_spec = pl.BlockSpec((1, GROUP_W, tm), lambda i: (i, 0, 0))
    tab_spec = pl.BlockSpec((tm, LANES), lambda i: (i % n_pos_tiles, 0))
    qk_shape = jax.ShapeDtypeStruct((t, GROUP_W), BF16)
    vt_shape = jax.ShapeDtypeStruct((t // tm, GROUP_W, tm), BF16)
    gate_w = N_GATE_GROUPS * GROUP_W
    return pl.pallas_call(
        _inproj_body,
        grid=(t // tm,),
        in_specs=[
            pl.BlockSpec((tm, d), lambda i: (i, 0)),
            pl.BlockSpec((1, d), lambda i: (0, 0)),
            pl.BlockSpec(w.shape, lambda i: (0, 0), pipeline_mode=pl.Buffered(1)),
            pl.BlockSpec(wvt.shape, lambda i: (0, 0, 0), pipeline_mode=pl.Buffered(1)),
            tab_spec, tab_spec, tab_spec,
        ],
        out_specs=[qk_spec, qk_spec, vt_spec] * 3 + [
            pl.BlockSpec((tm, gate_w), lambda i: (i, 0)),
            pl.BlockSpec((1, 1, GROUP_W), lambda i: (i, 0, 0)),
        ],
        out_shape=[qk_shape, qk_shape, vt_shape] * 3 + [
            jax.ShapeDtypeStruct((t, gate_w), F32),
            jax.ShapeDtypeStruct((t // tm, 1, GROUP_W), F32),
        ],
        compiler_params=_params(1),
        name="inproj",
    )(x, g, w, wvt, cos_t, sina_t, sinb_t)


def _stack_heads(q):
    lane = lax.broadcasted_iota(jnp.int32, q.shape, 1)
    zero = jnp.zeros_like(q)
    return jnp.concatenate([jnp.where(lane < HEAD_DIM, q, zero),
                            jnp.where(lane >= HEAD_DIM, q, zero)], axis=0)


def _unstack_heads(o_t, tq):
    return jnp.concatenate([o_t[:HEAD_DIM, :tq], o_t[HEAD_DIM:, tq:]], axis=0).T


def _tile_positions(tk, tq, k0, q0):
    key = lax.broadcasted_iota(jnp.int32, (tk, 2 * tq), 0) + k0
    col = lax.broadcasted_iota(jnp.int32, (tk, 2 * tq), 1)
    query = jnp.where(col >= tq, col - tq, col) + q0
    return key, query


def _pv(vt_ref, j0, p):
    ones = jnp.ones((SUM_ROWS, KEY_TILE), BF16)
    out = None
    for n in range(p.shape[0] // KEY_TILE):
        lhs = jnp.concatenate([vt_ref[j0 + n], ones], axis=0)
        part = _dot(lhs, p[n * KEY_TILE:(n + 1) * KEY_TILE])
        out = part if out is None else out + part
    return out


def _softmax_step(s, s_max, vt_ref, j0, m_ref, l_ref, acc_ref):
    m_prev = m_ref[...]
    m_new = jnp.maximum(m_prev, s_max)
    alpha = jnp.exp(m_prev - m_new)
    pv = _pv(vt_ref, j0, jnp.exp(s - m_new).astype(BF16))
    acc_ref[...] = alpha * acc_ref[...] + pv[:LANES]
    l_ref[...] = alpha * l_ref[...] + pv[LANES:LANES + 1]
    m_ref[...] = m_new


def _softmax_init(m_ref, l_ref, acc_ref):
    m_ref[...] = jnp.full_like(m_ref, MASKED)
    l_ref[...] = jnp.zeros_like(l_ref)
    acc_ref[...] = jnp.zeros_like(acc_ref)


def _col_max(s):
    return jnp.max(s, axis=0, keepdims=True)


def _pipelined_steps(n_regular, logits, update, final, s_ref, cmax_ref):
    def ahead(n, buf):
        s = logits(n)
        s_ref[buf] = s
        cmax_ref[buf] = _col_max(s)

    def consume(n, buf):
        update(n, s_ref[buf], cmax_ref[buf])

    odd = n_regular % 2
    s0 = logits(0)
    s_ref[odd] = s0
    cmax_ref[odd] = _col_max(s0)

    @pl.when(odd == 1)
    def _():
        ahead(1, 0)
        consume(0, 1)

    def pair(nn, carry):
        t = odd + 2 * nn
        ahead(t + 1, 1)
        consume(t, 0)
        ahead(t + 2, 0)
        consume(t + 1, 1)
        return carry

    lax.fori_loop(0, n_regular // 2, pair, 0)
    final(s_ref[0])


def _step_buffers(tq):
    return [pltpu.VMEM((2, tq, 2 * tq), F32), pltpu.VMEM((2, 1, 2 * tq), F32)]


def _attn_call(body, name, q, k, vt, extra_inputs, extra_specs, scratch, *, batch, seq, tq):
    rows_per_batch = seq // tq
    n_key_tiles = seq // KEY_TILE
    vt = vt.reshape(batch, n_key_tiles, GROUP_W, KEY_TILE)
    q_spec = pl.BlockSpec((tq, LANES), lambda b, h, i: (b * rows_per_batch + i, h))
    k_spec = pl.BlockSpec((seq, LANES), lambda b, h, i: (b, h))
    vt_spec = pl.BlockSpec((None, n_key_tiles, LANES, KEY_TILE), lambda b, h, i: (b, 0, h, 0))
    return pl.pallas_call(
        body,
        grid=(batch, N_GROUPS, rows_per_batch),
        in_specs=extra_specs + [q_spec, k_spec, vt_spec],
        out_specs=q_spec,
        out_shape=jax.ShapeDtypeStruct(q.shape, BF16),
        scratch_shapes=scratch,
        compiler_params=_params(3),
        name=name,
    )(*extra_inputs, q, k, vt)


def _row_stat(tq):
    return pltpu.VMEM((1, 2 * tq), F32)


def _acc(tq):
    return pltpu.VMEM((LANES, 2 * tq), F32)


def _diff_body(lam_ref, g_ref, q_ref, k_ref, vt_ref, o_ref, m_ref, l_ref, acc_ref, s_ref, cmax_ref,
               *, tq, post_scale):
    i = pl.program_id(2)
    qq = _stack_heads(q_ref[...])
    _softmax_init(m_ref, l_ref, acc_ref)
    tiles = tq // KEY_TILE

    def logits(n):
        k0 = pl.multiple_of(n * tq, tq)
        return _dot_nt(k_ref[pl.ds(k0, tq), :], qq)

    def update(n, s, s_max):
        _softmax_step(s, s_max, vt_ref, n * tiles, m_ref, l_ref, acc_ref)

    def final(s):
        key, query = _tile_positions(tq, tq, 0, 0)
        s = jnp.where(key <= query, s, MASKED)
        _softmax_step(s, _col_max(s), vt_ref, i * tiles, m_ref, l_ref, acc_ref)

    _pipelined_steps(i, logits, update, final, s_ref, cmax_ref)

    o_t = acc_ref[...] / l_ref[...]
    out = (o_t[:, :tq] - lam_ref[0, 0] * o_t[:, tq:]).T
    o_ref[...] = (_rms(out, g_ref[...]) * post_scale).astype(BF16)


def _diff_attention(q, k, vt, lam, gain, *, batch, seq, post_scale, tq=ATTN_TQ):
    body = functools.partial(_diff_body, tq=tq, post_scale=post_scale)
    return _attn_call(
        body, "diff_attn", q, k, vt,
        [lam, gain],
        [pl.BlockSpec(memory_space=pltpu.SMEM), pl.BlockSpec((1, LANES), lambda b, h, i: (0, 0))],
        [_row_stat(tq), _row_stat(tq), _acc(tq)] + _step_buffers(tq),
        batch=batch, seq=seq, tq=tq)


def _sb_body(q_ref, k_ref, vt_ref, o_ref, c_ref, acc_ref, *, tq):
    tk = KEY_TILE
    i = pl.program_id(2)
    qq = _stack_heads(q_ref[...])
    c_ref[...] = jnp.zeros_like(c_ref)
    acc_ref[...] = jnp.zeros_like(acc_ref)
    later = (lax.broadcasted_iota(jnp.int32, (tk, tk), 1)
             > lax.broadcasted_iota(jnp.int32, (tk, tk), 0)).astype(BF16)

    def block(j, masked):
        k0 = pl.multiple_of(j * tk, tk)
        z = _dot_nt(k_ref[pl.ds(k0, tk), :], qq)
        softplus = jnp.maximum(z, 0.0) + jnp.log1p(jnp.exp(-jnp.abs(z)))
        log_keep = -softplus
        if masked:
            key, query = _tile_positions(tk, tq, k0, i * tq)
            valid = key < query
            log_keep = jnp.where(valid, log_keep, 0.0)
        hi = log_keep.astype(BF16)
        lo = (log_keep - hi.astype(F32)).astype(BF16)
        c = c_ref[...]
        log_w = (z - softplus) + (_dot(later, hi) + _dot(later, lo)) + c
        w = jnp.exp(log_w)
        if masked:
            w = jnp.where(valid, w, 0.0)
        acc_ref[...] += _dot(vt_ref[j], w.astype(BF16))
        c_new = c + jnp.sum(log_keep, axis=0, keepdims=True)
        c_ref[...] = c_new
        return jnp.max(c_new)

    def cond(state):
        j, c_max = state
        return jnp.logical_and(j >= 0, c_max > SB_LOG_ZERO)

    def body(state):
        j, _ = state
        return j - 1, block(j, False)

    lax.while_loop(cond, body, (i - 1, block(i, True)))
    o_ref[...] = _unstack_heads(acc_ref[...], tq).astype(BF16)


def _sb_attention(q, k, vt, *, batch, seq):
    tq = KEY_TILE
    body = functools.partial(_sb_body, tq=tq)
    return _attn_call(body, "sb_attn", q, k, vt, [], [], [_row_stat(tq), _acc(tq)],
                      batch=batch, seq=seq, tq=tq)


def _moba_body(km_ref, q_ref, k_ref, vt_ref, o_ref, m_ref, l_ref, acc_ref, bias_ref, s_ref, cmax_ref,
               *, tq):
    i = pl.program_id(2)
    tk = MOBA_BLOCK
    blocks = tq // tk
    qq = _stack_heads(q_ref[...])

    gate = _dot_nt(km_ref[0], qq)
    blk = lax.broadcasted_iota(jnp.int32, gate.shape, 0)
    col = lax.broadcasted_iota(jnp.int32, (1, 2 * tq), 1)
    own = i * blocks + jnp.where(col >= tq, col - tq, col) // tk
    gate = jnp.where(blk < own, gate, -jnp.inf)
    selected = blk == own
    for _ in range(MOBA_TOPK):
        best = jnp.max(gate, axis=0, keepdims=True)
        idx = jnp.min(jnp.where(gate == best, blk, LANES), axis=0, keepdims=True)
        pick = jnp.logical_and(blk == idx, best > -jnp.inf)
        selected = jnp.logical_or(selected, pick)
        gate = jnp.where(blk == idx, -jnp.inf, gate)
    bias_ref[...] = jnp.where(selected, 0.0, MASKED)

    _softmax_init(m_ref, l_ref, acc_ref)

    def logits(n):
        k0 = pl.multiple_of(n * tq, tq)
        s = _dot_nt(k_ref[pl.ds(k0, tq), :], qq)
        return jnp.concatenate([s[g * tk:(g + 1) * tk] + bias_ref[pl.ds(n * blocks + g, 1), :]
                                for g in range(blocks)], axis=0)

    def update(n, s, s_max):
        _softmax_step(s, s_max, vt_ref, n * blocks, m_ref, l_ref, acc_ref)

    def final(s):
        key, query = _tile_positions(tq, tq, 0, 0)
        s = jnp.where(key <= query, s, MASKED)
        _softmax_step(s, _col_max(s), vt_ref, i * blocks, m_ref, l_ref, acc_ref)

    _pipelined_steps(i, logits, update, final, s_ref, cmax_ref)
    o_ref[...] = _unstack_heads(acc_ref[...] / l_ref[...], tq).astype(BF16)


def _moba_attention(q, k, vt, kmean, *, batch, seq, tq=ATTN_TQ):
    body = functools.partial(_moba_body, tq=tq)
    return _attn_call(
        body, "moba_attn", q, k, vt,
        [kmean],
        [pl.BlockSpec((1, LANES, LANES), lambda b, h, i: (b, 0, h))],
        [_row_stat(tq), _row_stat(tq), _acc(tq), _acc(tq)] + _step_buffers(tq),
        batch=batch, seq=seq, tq=tq)


def _merge_body(x_ref, a_ref, b_ref, c_ref, gate_ref, wa_ref, wb_ref, wc_ref, wo_ref, g_ref, o_ref):
    d = x_ref.shape[1]
    merged = (gate_ref[:, 0:d] * _dot(a_ref[...], wa_ref[...])
              + gate_ref[:, d:2 * d] * _dot(b_ref[...], wb_ref[...])
              + gate_ref[:, 2 * d:3 * d] * _dot(c_ref[...], wc_ref[...]))
    y = _dot(merged.astype(BF16), wo_ref[...])
    o_ref[...] = x_ref[...] + _rms(y, g_ref[...])


def _merge(x, a, b, c, gates, wa, wb, wc, wo, g, *, tm=512):
    t, d = x.shape
    row = lambda w: pl.BlockSpec((tm, w), lambda i: (i, 0))
    whole = lambda arr: pl.BlockSpec(arr.shape, lambda i: (0, 0))
    return pl.pallas_call(
        _merge_body,
        grid=(t // tm,),
        in_specs=[row(d), row(GROUP_W), row(GROUP_W), row(GROUP_W), row(3 * d),
                  whole(wa), whole(wb), whole(wc), whole(wo), whole(g)],
        out_specs=row(d),
        out_shape=jax.ShapeDtypeStruct((t, d), F32),
        compiler_params=_params(1),
        name="merge",
    )(x, a, b, c, gates, wa, wb, wc, wo, g)


def _rope_lane_tables(seq):
    pos = jnp.arange(seq, dtype=F32)
    inv_freq = ROPE_THETA ** (-jnp.arange(0, ROT_DIM, 2, dtype=F32) / ROT_DIM)
    ang = pos[:, None] * inv_freq[None, :]
    cos, sin = jnp.cos(ang), jnp.sin(ang)
    half = ROT_DIM // 2
    rest = HEAD_DIM - ROT_DIM
    cos_h = jnp.concatenate([cos, cos, jnp.ones((seq, rest), F32)], axis=1)
    sina_h = jnp.concatenate([-sin, jnp.zeros((seq, half + rest), F32)], axis=1)
    sinb_h = jnp.concatenate([jnp.zeros((seq, half), F32), sin, jnp.zeros((seq, rest), F32)], axis=1)
    reps = LANES // HEAD_DIM
    return tuple(jnp.tile(t, (1, reps)) for t in (cos_h, sina_h, sinb_h))


def kernel(x, w_in, w_diff_o, w_sb_o, w_moba_o, w_out, lam_q1, lam_k1, lam_q2, lam_k2,
           diff_norm_g, ffn1_wg, ffn1_wu, ffn1_wd, ffn2_wg, ffn2_wu, ffn2_wd,
           g_ffn1_pre, g_ffn1_post, g_mix_pre, g_mix_post, g_ffn2_pre, g_ffn2_post):
    batch, seq, d = x.shape
    depth = w_in.shape[0]
    n_blocks = seq // MOBA_BLOCK
    assert seq % 512 == 0 and n_blocks <= LANES
    cos_t, sina_t, sinb_t = _rope_lane_tables(seq)
    bf = lambda w: w.astype(BF16)
    vec = lambda g: g.reshape(1, -1)
    xt = x.reshape(batch * seq, d)
    attn = dict(batch=batch, seq=seq)

    for l in range(depth):
        lambda_init = 0.8 - 0.6 * math.exp(-0.3 * l)
        xt = _ffn(xt, vec(g_ffn1_pre[l]), bf(ffn1_wg[l]), bf(ffn1_wu[l]), bf(ffn1_wd[l]),
                  vec(g_ffn1_post[l]))

        w_l = bf(w_in[l])
        w_vt = jnp.stack([w_l[:, g * GROUP_W:(g + 1) * GROUP_W].T for g in V_GROUPS])
        (dq, dk, dvt, sq, sk, svt, mq, mk, mvt, gates, kmean) = _inproj(
            xt, vec(g_mix_pre[l]), w_l, w_vt, cos_t, sina_t, sinb_t, seq)
        lam = (jnp.exp(jnp.sum(lam_q1[l] * lam_k1[l])) - jnp.exp(jnp.sum(lam_q2[l] * lam_k2[l]))
               + lambda_init).reshape(1, 1)
        a_out = _diff_attention(dq, dk, dvt, lam, vec(diff_norm_g[l]),
                                post_scale=1.0 - lambda_init, **attn)
        b_out = _sb_attention(sq, sk, svt, **attn)
        kmean = kmean.reshape(batch, n_blocks, GROUP_W)
        kmean = jnp.pad(kmean, ((0, 0), (0, LANES - n_blocks), (0, 0))).astype(BF16)
        c_out = _moba_attention(mq, mk, mvt, kmean, **attn)

        xt = _merge(xt, a_out, b_out, c_out, gates, bf(w_diff_o[l]), bf(w_sb_o[l]), bf(w_moba_o[l]),
                    bf(w_out[l]), vec(g_mix_post[l]))

        xt = _ffn(xt, vec(g_ffn2_pre[l]), bf(ffn2_wg[l]), bf(ffn2_wu[l]), bf(ffn2_wd[l]),
                  vec(g_ffn2_post[l]))
    return xt.reshape(batch, seq, d)
```

```python
import functools
import math

import jax
import jax.numpy as jnp
from jax import lax
from jax.experimental import pallas as pl
from jax.experimental.pallas import tpu as pltpu

F32 = jnp.float32
BF16 = jnp.bfloat16

LANES = 128
HEAD_DIM = 64
ROT_DIM = HEAD_DIM // 4
ROPE_THETA = 500000.0
MOBA_BLOCK = 256
MOBA_TOPK = 3
NORM_EPS = 1e-6
GROUP_W = 512
N_GROUPS = GROUP_W // LANES
N_QKV_GROUPS = 9
N_GATE_GROUPS = 6
V_GROUPS = (2, 5, 8)
KEY_TILE = MOBA_BLOCK
SB_LOG_ZERO = -104.0
MASKED = -1e30
BF16_ROWS = 16
SUM_ROWS = BF16_ROWS
ATTN_TQ = 512
VMEM_LIMIT = 56 * 1024 * 1024

_NT = (((1,), (1,)), ((), ()))


def _dot(a, b):
    return jnp.dot(a, b, preferred_element_type=F32)


def _dot_nt(a, b):
    return lax.dot_general(a, b, _NT, preferred_element_type=F32)


def _sigmoid(x):
    return 1.0 / (1.0 + jnp.exp(-x))


def _rms(x, gain):
    return x * lax.rsqrt(jnp.mean(x * x, axis=-1, keepdims=True) + NORM_EPS) * gain


def _params(n_axes):
    return pltpu.CompilerParams(dimension_semantics=("arbitrary",) * n_axes,
                                vmem_limit_bytes=VMEM_LIMIT)


def _ffn_body(x_ref, gpre_ref, wg_ref, wu_ref, wd_ref, gpost_ref, o_ref, xn_ref, acc_ref):
    f = pl.program_id(1)

    @pl.when(f == 0)
    def _():
        xn_ref[...] = _rms(x_ref[...], gpre_ref[...]).astype(BF16)
        acc_ref[...] = jnp.zeros_like(acc_ref)

    xn = xn_ref[...]
    g = _dot(xn, wg_ref[...])
    u = _dot(xn, wu_ref[...])
    a = (g * _sigmoid(g)) * u
    acc_ref[...] += _dot(a.astype(BF16), wd_ref[...])

    @pl.when(f == pl.num_programs(1) - 1)
    def _():
        o_ref[...] = x_ref[...] + 0.5 * _rms(acc_ref[...], gpost_ref[...])


def _ffn(x, gpre, wg, wu, wd, gpost, *, tm=512, tf=1408):
    t, d = x.shape
    f = wg.shape[1]
    return pl.pallas_call(
        _ffn_body,
        grid=(t // tm, f // tf),
        in_specs=[
            pl.BlockSpec((tm, d), lambda i, j: (i, 0)),
            pl.BlockSpec((1, d), lambda i, j: (0, 0)),
            pl.BlockSpec((d, tf), lambda i, j: (0, j)),
            pl.BlockSpec((d, tf), lambda i, j: (0, j)),
            pl.BlockSpec((tf, d), lambda i, j: (j, 0)),
            pl.BlockSpec((1, d), lambda i, j: (0, 0)),
        ],
        out_specs=pl.BlockSpec((tm, d), lambda i, j: (i, 0)),
        out_shape=jax.ShapeDtypeStruct((t, d), F32),
        scratch_shapes=[pltpu.VMEM((tm, d), BF16), pltpu.VMEM((tm, d), F32)],
        compiler_params=_params(2),
        name="ffn",
    )(x, gpre, wg, wu, wd, gpost)


def _inproj_body(x_ref, g_ref, w_ref, wvt_ref, c_ref, sa_ref, sb_ref,
                 dq_ref, dk_ref, dvt_ref, sq_ref, sk_ref, svt_ref, mq_ref, mk_ref, mvt_ref,
                 gate_ref, kmean_ref):
    xn = _rms(x_ref[...], g_ref[...]).astype(BF16)
    cos, sin_a, sin_b = c_ref[...], sa_ref[...], sb_ref[...]
    scale = HEAD_DIM ** -0.5

    def proj(group):
        return _dot(xn, w_ref[:, group * GROUP_W:(group + 1) * GROUP_W])

    def rope(t):
        half = ROT_DIM // 2
        return (t * cos + pltpu.roll(t, LANES - half, 1) * sin_a
                + pltpu.roll(t, half, 1) * sin_b)

    def emit(out_ref, group, roped, scaled):
        t = proj(group)
        slabs = []
        for s in range(N_GROUPS):
            ts = t[:, s * LANES:(s + 1) * LANES]
            if roped:
                ts = rope(ts)
            slabs.append(ts)
            out_ref[:, s * LANES:(s + 1) * LANES] = (ts * scale if scaled else ts).astype(BF16)
        return slabs

    emit(dq_ref, 0, True, True)
    emit(dk_ref, 1, True, False)
    emit(sq_ref, 3, False, True)
    emit(sk_ref, 4, False, False)
    emit(mq_ref, 6, True, True)
    mk_slabs = emit(mk_ref, 7, True, False)
    for s, ts in enumerate(mk_slabs):
        kmean_ref[0, :, s * LANES:(s + 1) * LANES] = jnp.mean(ts, axis=0, keepdims=True)
    for n, vt_ref in enumerate((dvt_ref, svt_ref, mvt_ref)):
        vt_ref[0] = _dot_nt(wvt_ref[n], xn).astype(BF16)
    for gi in range(N_GATE_GROUPS):
        gate_ref[:, gi * GROUP_W:(gi + 1) * GROUP_W] = _sigmoid(proj(N_QKV_GROUPS + gi))


def _inproj(x, g, w, wvt, cos_t, sina_t, sinb_t, seq):
    t, d = x.shape
    tm = KEY_TILE
    n_pos_tiles = seq // tm
    qk_spec = pl.BlockSpec((tm, GROUP_W), lambda i: (i, 0))
    vt_spec = pl.BlockSpec((1, GROUP_W, tm), lambda i: (i, 0, 0))
    tab_spec = pl.BlockSpec((tm, LANES), lambda i: (i % n_pos_tiles, 0))
    qk_shape = jax.ShapeDtypeStruct((t, GROUP_W), BF16)
    vt_shape = jax.ShapeDtypeStruct((t // tm, GROUP_W, tm), BF16)
    gate_w = N_GATE_GROUPS * GROUP_W
    return pl.pallas_call(
        _inproj_body,
        grid=(t // tm,),
        in_specs=[
            pl.BlockSpec((tm, d), lambda i: (i, 0)),
            pl.BlockSpec((1, d), lambda i: (0, 0)),
            pl.BlockSpec(w.shape, lambda i: (0, 0), pipeline_mode=pl.Buffered(1)),
            pl.BlockSpec(wvt.shape, lambda i: (0, 0, 0), pipeline_mode=pl.Buffered(1)),
            tab_spec, tab_spec, tab_spec,
        ],
        out_specs=[qk_spec, qk_spec, vt_spec] * 3 + [
            pl.BlockSpec((tm, gate_w), lambda i: (i, 0)),
            pl.BlockSpec((1, 1, GROUP_W), lambda i: (i, 0, 0)),
        ],
        out_shape=[qk_shape, qk_shape, vt_shape] * 3 + [
            jax.ShapeDtypeStruct((t, gate_w), F32),
            jax.ShapeDtypeStruct((t // tm, 1, GROUP_W), F32),
        ],
        compiler_params=_params(1),
        name="inproj",
    )(x, g, w, wvt, cos_t, sina_t, sinb_t)


def _stack_heads(q):
    lane = lax.broadcasted_iota(jnp.int32, q.shape, 1)
    zero = jnp.zeros_like(q)
    return jnp.concatenate([jnp.where(lane < HEAD_DIM, q, zero),
                            jnp.where(lane >= HEAD_DIM, q, zero)], axis=0)


def _unstack_heads(o_t, tq):
    return jnp.concatenate([o_t[:HEAD_DIM, :tq], o_t[HEAD_DIM:, tq:]], axis=0).T


def _tile_positions(tk, tq, k0, q0):
    key = lax.broadcasted_iota(jnp.int32, (tk, 2 * tq), 0) + k0
    col = lax.broadcasted_iota(jnp.int32, (tk, 2 * tq), 1)
    query = jnp.where(col >= tq, col - tq, col) + q0
    return key, query


def _pv(vt_ref, j0, p, per_head):
    ones = jnp.ones((SUM_ROWS, KEY_TILE), BF16)
    half = p.shape[1] // 2
    out = None
    for n in range(p.shape[0] // KEY_TILE):
        vt = vt_ref[j0 + n]
        pn = p[n * KEY_TILE:(n + 1) * KEY_TILE]
        if per_head:
            part = jnp.concatenate(
                [_dot(jnp.concatenate([vt[:HEAD_DIM], ones], axis=0), pn[:, :half]),
                 _dot(jnp.concatenate([vt[HEAD_DIM:], ones], axis=0), pn[:, half:])], axis=1)
        else:
            part = _dot(jnp.concatenate([vt, ones], axis=0), pn)
        out = part if out is None else out + part
    return out


def _softmax_step(s, s_max, vt_ref, j0, m_ref, acc_ref, per_head):
    m_prev = m_ref[...]
    m_new = jnp.maximum(m_prev, s_max)
    alpha = jnp.exp(m_prev - m_new)
    acc_ref[...] = alpha * acc_ref[...] + _pv(vt_ref, j0, jnp.exp(s - m_new).astype(BF16), per_head)
    m_ref[...] = m_new


def _softmax_init(m_ref, acc_ref):
    m_ref[...] = jnp.full_like(m_ref, MASKED)
    acc_ref[...] = jnp.zeros_like(acc_ref)


def _softmax_result(acc_ref, n_features):
    return acc_ref[:n_features] / acc_ref[n_features:n_features + 1]


def _col_max(s):
    return jnp.max(s, axis=0, keepdims=True)


def _pipelined_steps(n_regular, logits, update, final, s_ref, cmax_ref):
    def ahead(n, buf):
        s = logits(n)
        s_ref[buf] = s
        cmax_ref[buf] = _col_max(s)

    def consume(n, buf):
        update(n, s_ref[buf], cmax_ref[buf])

    odd = n_regular % 2
    s0 = logits(0)
    s_ref[odd] = s0
    cmax_ref[odd] = _col_max(s0)

    @pl.when(odd == 1)
    def _():
        ahead(1, 0)
        consume(0, 1)

    def pair(nn, carry):
        t = odd + 2 * nn
        ahead(t + 1, 1)
        consume(t, 0)
        ahead(t + 2, 0)
        consume(t + 1, 1)
        return carry

    lax.fori_loop(0, n_regular // 2, pair, 0)
    final(s_ref[0])


def _step_buffers(tq):
    return [pltpu.VMEM((2, tq, 2 * tq), F32), pltpu.VMEM((2, 1, 2 * tq), F32)]


def _attn_call(body, name, q, k, vt, extra_inputs, extra_specs, scratch, *, batch, seq, tq):
    rows_per_batch = seq // tq
    n_key_tiles = seq // KEY_TILE
    vt = vt.reshape(batch, n_key_tiles, GROUP_W, KEY_TILE)
    q_spec = pl.BlockSpec((tq, LANES), lambda b, h, i: (b * rows_per_batch + i, h))
    k_spec = pl.BlockSpec((seq, LANES), lambda b, h, i: (b, h))
    vt_spec = pl.BlockSpec((None, n_key_tiles, LANES, KEY_TILE), lambda b, h, i: (b, 0, h, 0))
    return pl.pallas_call(
        body,
        grid=(batch, N_GROUPS, rows_per_batch),
        in_specs=extra_specs + [q_spec, k_spec, vt_spec],
        out_specs=q_spec,
        out_shape=jax.ShapeDtypeStruct(q.shape, BF16),
        scratch_shapes=scratch,
        compiler_params=_params(3),
        name=name,
    )(*extra_inputs, q, k, vt)


def _row_stat(tq):
    return pltpu.VMEM((1, 2 * tq), F32)


def _acc(tq, n_features=LANES):
    return pltpu.VMEM((n_features, 2 * tq), F32)


def _diff_body(lam_ref, g_ref, q_ref, k_ref, vt_ref, o_ref, m_ref, acc_ref, s_ref, cmax_ref,
               *, tq, post_scale):
    i = pl.program_id(2)
    qq = _stack_heads(q_ref[...])
    _softmax_init(m_ref, acc_ref)
    tiles = tq // KEY_TILE

    def logits(n):
        k0 = pl.multiple_of(n * tq, tq)
        return _dot_nt(k_ref[pl.ds(k0, tq), :], qq)

    def update(n, s, s_max):
        _softmax_step(s, s_max, vt_ref, n * tiles, m_ref, acc_ref, False)

    def final(s):
        key, query = _tile_positions(tq, tq, 0, 0)
        s = jnp.where(key <= query, s, MASKED)
        _softmax_step(s, _col_max(s), vt_ref, i * tiles, m_ref, acc_ref, False)

    _pipelined_steps(i, logits, update, final, s_ref, cmax_ref)

    o_t = _softmax_result(acc_ref, LANES)
    out = (o_t[:, :tq] - lam_ref[0, 0] * o_t[:, tq:]).T
    o_ref[...] = (_rms(out, g_ref[...]) * post_scale).astype(BF16)


def _diff_attention(q, k, vt, lam, gain, *, batch, seq, post_scale, tq=ATTN_TQ):
    body = functools.partial(_diff_body, tq=tq, post_scale=post_scale)
    return _attn_call(
        body, "diff_attn", q, k, vt,
        [lam, gain],
        [pl.BlockSpec(memory_space=pltpu.SMEM), pl.BlockSpec((1, LANES), lambda b, h, i: (0, 0))],
        [_row_stat(tq), _acc(tq, LANES + SUM_ROWS)] + _step_buffers(tq),
        batch=batch, seq=seq, tq=tq)


def _sb_body(q_ref, k_ref, vt_ref, o_ref, c_ref, acc_ref, *, tq):
    tk = KEY_TILE
    i = pl.program_id(2)
    qq = _stack_heads(q_ref[...])
    row = lax.broadcasted_iota(jnp.int32, (tk + SUM_ROWS, tk), 0)
    later = jnp.logical_or(lax.broadcasted_iota(jnp.int32, row.shape, 1) > row, row >= tk).astype(BF16)

    def tile_terms(j, valid):
        k0 = pl.multiple_of(j * tk, tk)
        z = _dot_nt(k_ref[pl.ds(k0, tk), :], qq)
        softplus = jnp.maximum(z, 0.0) + jnp.log(1.0 + jnp.exp(-jnp.abs(z)))
        log_keep = -softplus
        if valid is not None:
            log_keep = jnp.where(valid, log_keep, 0.0)
        hi = log_keep.astype(BF16)
        lo = (log_keep - hi.astype(F32)).astype(BF16)
        sums = _dot(later, hi) + _dot(later, lo)
        return (z - softplus) + sums[:tk], sums[tk:tk + 1]

    key, query = _tile_positions(tk, tq, 0, 0)
    valid = key < query
    log_w, total = tile_terms(i, valid)
    w = jnp.where(valid, jnp.exp(log_w), 0.0)
    has_prev = i > 0
    prev = jnp.maximum(i - 1, 0)
    log_w_prev, total_prev = tile_terms(prev, None)
    w_prev = jnp.exp(log_w_prev + total)
    vt_prev = vt_ref[prev]
    vt_prev = jnp.where(has_prev, vt_prev, jnp.zeros_like(vt_prev))
    acc_ref[...] = _dot(vt_ref[i], w.astype(BF16)) + _dot(vt_prev, w_prev.astype(BF16))
    c = total + jnp.where(has_prev, total_prev, 0.0)
    c_ref[...] = c

    def cond(state):
        j, c_max = state
        return jnp.logical_and(j >= 0, c_max > SB_LOG_ZERO)

    def body(state):
        j, _ = state
        log_w, total = tile_terms(j, None)
        c = c_ref[...]
        acc_ref[...] += _dot(vt_ref[j], jnp.exp(log_w + c).astype(BF16))
        c_ref[...] = c + total
        return j - 1, jnp.max(c + total)

    lax.while_loop(cond, body, (i - 2, jnp.max(c)))
    o_ref[...] = _unstack_heads(acc_ref[...], tq).astype(BF16)


def _sb_attention(q, k, vt, *, batch, seq):
    tq = KEY_TILE
    body = functools.partial(_sb_body, tq=tq)
    return _attn_call(body, "sb_attn", q, k, vt, [], [], [_row_stat(tq), _acc(tq)],
                      batch=batch, seq=seq, tq=tq)


def _moba_body(km_ref, q_ref, k_ref, vt_ref, o_ref, m_ref, acc_ref, s_ref, cmax_ref, *, tq):
    i = pl.program_id(2)
    tk = MOBA_BLOCK
    blocks = tq // tk
    qq = _stack_heads(q_ref[...])

    gate = _dot_nt(km_ref[0], qq)
    blk = lax.broadcasted_iota(jnp.int32, gate.shape, 0)
    col = lax.broadcasted_iota(jnp.int32, (1, 2 * tq), 1)
    own = i * blocks + jnp.where(col >= tq, col - tq, col) // tk
    gate = jnp.where(blk < own, gate, -jnp.inf)
    selected = blk == own
    for _ in range(MOBA_TOPK):
        best = jnp.max(gate, axis=0, keepdims=True)
        idx = jnp.min(jnp.where(gate == best, blk, LANES), axis=0, keepdims=True)
        pick = jnp.logical_and(blk == idx, best > -jnp.inf)
        selected = jnp.logical_or(selected, pick)
        gate = jnp.where(blk == idx, -jnp.inf, gate)
    bias = jnp.where(selected, 0.0, MASKED)
    bias = jnp.concatenate([bias, jnp.zeros((LANES - bias.shape[0], 2 * tq), F32)], axis=0)
    qq = jnp.concatenate([qq, bias.T.astype(BF16)], axis=1)
    key_blk = lax.broadcasted_iota(jnp.int32, (tq, LANES), 0) // tk
    lane = lax.broadcasted_iota(jnp.int32, (tq, LANES), 1)

    _softmax_init(m_ref, acc_ref)

    def logits(n):
        k0 = pl.multiple_of(n * tq, tq)
        one_hot = (lane == key_blk + n * blocks).astype(BF16)
        return _dot_nt(jnp.concatenate([k_ref[pl.ds(k0, tq), :], one_hot], axis=1), qq)

    def update(n, s, s_max):
        _softmax_step(s, s_max, vt_ref, n * blocks, m_ref, acc_ref, True)

    def final(s):
        key, query = _tile_positions(tq, tq, 0, 0)
        s = jnp.where(key <= query, s, MASKED)
        _softmax_step(s, _col_max(s), vt_ref, i * blocks, m_ref, acc_ref, True)

    _pipelined_steps(i, logits, update, final, s_ref, cmax_ref)
    o_t = _softmax_result(acc_ref, HEAD_DIM)
    o_ref[...] = jnp.concatenate([o_t[:, :tq], o_t[:, tq:]], axis=0).T.astype(BF16)


def _moba_attention(q, k, vt, kmean, *, batch, seq, tq=ATTN_TQ):
    body = functools.partial(_moba_body, tq=tq)
    return _attn_call(
        body, "moba_attn", q, k, vt,
        [kmean],
        [pl.BlockSpec((1, kmean.shape[1], LANES), lambda b, h, i: (b, 0, h))],
        [_row_stat(tq), _acc(tq, HEAD_DIM + SUM_ROWS)] + _step_buffers(tq),
        batch=batch, seq=seq, tq=tq)


def _merge_body(x_ref, a_ref, b_ref, c_ref, gate_ref, wa_ref, wb_ref, wc_ref, wo_ref, g_ref, o_ref):
    d = x_ref.shape[1]
    merged = (gate_ref[:, 0:d] * _dot(a_ref[...], wa_ref[...])
              + gate_ref[:, d:2 * d] * _dot(b_ref[...], wb_ref[...])
              + gate_ref[:, 2 * d:3 * d] * _dot(c_ref[...], wc_ref[...]))
    y = _dot(merged.astype(BF16), wo_ref[...])
    o_ref[...] = x_ref[...] + _rms(y, g_ref[...])


def _merge(x, a, b, c, gates, wa, wb, wc, wo, g, *, tm=512):
    t, d = x.shape
    row = lambda w: pl.BlockSpec((tm, w), lambda i: (i, 0))
    whole = lambda arr: pl.BlockSpec(arr.shape, lambda i: (0, 0))
    return pl.pallas_call(
        _merge_body,
        grid=(t // tm,),
        in_specs=[row(d), row(GROUP_W), row(GROUP_W), row(GROUP_W), row(3 * d),
                  whole(wa), whole(wb), whole(wc), whole(wo), whole(g)],
        out_specs=row(d),
        out_shape=jax.ShapeDtypeStruct((t, d), F32),
        compiler_params=_params(1),
        name="merge",
    )(x, a, b, c, gates, wa, wb, wc, wo, g)


def _rope_lane_tables(seq):
    pos = jnp.arange(seq, dtype=F32)
    inv_freq = ROPE_THETA ** (-jnp.arange(0, ROT_DIM, 2, dtype=F32) / ROT_DIM)
    ang = pos[:, None] * inv_freq[None, :]
    cos, sin = jnp.cos(ang), jnp.sin(ang)
    half = ROT_DIM // 2
    rest = HEAD_DIM - ROT_DIM
    cos_h = jnp.concatenate([cos, cos, jnp.ones((seq, rest), F32)], axis=1)
    sina_h = jnp.concatenate([-sin, jnp.zeros((seq, half + rest), F32)], axis=1)
    sinb_h = jnp.concatenate([jnp.zeros((seq, half), F32), sin, jnp.zeros((seq, rest), F32)], axis=1)
    reps = LANES // HEAD_DIM
    return tuple(jnp.tile(t, (1, reps)) for t in (cos_h, sina_h, sinb_h))


def kernel(x, w_in, w_diff_o, w_sb_o, w_moba_o, w_out, lam_q1, lam_k1, lam_q2, lam_k2,
           diff_norm_g, ffn1_wg, ffn1_wu, ffn1_wd, ffn2_wg, ffn2_wu, ffn2_wd,
           g_ffn1_pre, g_ffn1_post, g_mix_pre, g_mix_post, g_ffn2_pre, g_ffn2_post):
    batch, seq, d = x.shape
    depth = w_in.shape[0]
    n_blocks = seq // MOBA_BLOCK
    assert seq % 512 == 0 and n_blocks <= LANES
    cos_t, sina_t, sinb_t = _rope_lane_tables(seq)
    bf = lambda w: w.astype(BF16)
    vec = lambda g: g.reshape(1, -1)
    xt = x.reshape(batch * seq, d)
    attn = dict(batch=batch, seq=seq)

    for l in range(depth):
        lambda_init = 0.8 - 0.6 * math.exp(-0.3 * l)
        xt = _ffn(xt, vec(g_ffn1_pre[l]), bf(ffn1_wg[l]), bf(ffn1_wu[l]), bf(ffn1_wd[l]),
                  vec(g_ffn1_post[l]))

        w_l = bf(w_in[l])
        w_vt = jnp.stack([w_l[:, g * GROUP_W:(g + 1) * GROUP_W].T for g in V_GROUPS])
        (dq, dk, dvt, sq, sk, svt, mq, mk, mvt, gates, kmean) = _inproj(
            xt, vec(g_mix_pre[l]), w_l, w_vt, cos_t, sina_t, sinb_t, seq)
        lam = (jnp.exp(jnp.sum(lam_q1[l] * lam_k1[l])) - jnp.exp(jnp.sum(lam_q2[l] * lam_k2[l]))
               + lambda_init).reshape(1, 1)
        a_out = _diff_attention(dq, dk, dvt, lam, vec(diff_norm_g[l]),
                                post_scale=1.0 - lambda_init, **attn)
        b_out = _sb_attention(sq, sk, svt, **attn)
        kmean = kmean.reshape(batch, n_blocks, GROUP_W)
        kmean = jnp.pad(kmean, ((0, 0), (0, -n_blocks % BF16_ROWS), (0, 0))).astype(BF16)
        c_out = _moba_attention(mq, mk, mvt, kmean, **attn)

        xt = _merge(xt, a_out, b_out, c_out, gates, bf(w_diff_o[l]), bf(w_sb_o[l]), bf(w_moba_o[l]),
                    bf(w_out[l]), vec(g_mix_post[l]))

        xt = _ffn(xt, vec(g_ffn2_pre[l]), bf(ffn2_wg[l]), bf(ffn2_wu[l]), bf(ffn2_wd[l]),
                  vec(g_ffn2_post[l]))
    return xt.reshape(batch, seq, d)
```

```python
import functools
import math

import jax
import jax.numpy as jnp
from jax import lax
from jax.experimental import pallas as pl
from jax.experimental.pallas import tpu as pltpu

F32 = jnp.float32
BF16 = jnp.bfloat16

LANES = 128
HEAD_DIM = 64
ROT_DIM = HEAD_DIM // 4
ROPE_THETA = 500000.0
MOBA_BLOCK = 256
MOBA_TOPK = 3
NORM_EPS = 1e-6
GROUP_W = 512
N_GROUPS = GROUP_W // LANES
N_QKV_GROUPS = 9
N_GATE_GROUPS = 6
V_GROUPS = (2, 5, 8)
KEY_TILE = MOBA_BLOCK
SB_LOG_ZERO = -104.0
MASKED = -1e30
BF16_ROWS = 16
SUM_ROWS = BF16_ROWS
ATTN_TQ = 512
VMEM_LIMIT = 56 * 1024 * 1024

_NT = (((1,), (1,)), ((), ()))
_TN = (((0,), (1,)), ((), ()))


def _dot(a, b):
    return jnp.dot(a, b, preferred_element_type=F32)


def _dot_nt(a, b):
    return lax.dot_general(a, b, _NT, preferred_element_type=F32)


def _sigmoid(x):
    return 1.0 / (1.0 + jnp.exp(-x))


def _rms(x, gain):
    return x * lax.rsqrt(jnp.mean(x * x, axis=-1, keepdims=True) + NORM_EPS) * gain


def _params(n_axes):
    return pltpu.CompilerParams(dimension_semantics=("arbitrary",) * n_axes,
                                vmem_limit_bytes=VMEM_LIMIT)


def _ffn_body(x_ref, gpre_ref, wg_ref, wu_ref, wd_ref, gpost_ref, o_ref, xn_ref, acc_ref):
    f = pl.program_id(1)

    @pl.when(f == 0)
    def _():
        xn_ref[...] = _rms(x_ref[...], gpre_ref[...]).astype(BF16)
        acc_ref[...] = jnp.zeros_like(acc_ref)

    xn = xn_ref[...]
    g = _dot(xn, wg_ref[...])
    u = _dot(xn, wu_ref[...])
    a = (g * _sigmoid(g)) * u
    acc_ref[...] += _dot(a.astype(BF16), wd_ref[...])

    @pl.when(f == pl.num_programs(1) - 1)
    def _():
        o_ref[...] = x_ref[...] + 0.5 * _rms(acc_ref[...], gpost_ref[...])


def _ffn(x, gpre, wg, wu, wd, gpost, *, tm=512, tf=1408):
    t, d = x.shape
    f = wg.shape[1]
    return pl.pallas_call(
        _ffn_body,
        grid=(t // tm, f // tf),
        in_specs=[
            pl.BlockSpec((tm, d), lambda i, j: (i, 0)),
            pl.BlockSpec((1, d), lambda i, j: (0, 0)),
            pl.BlockSpec((d, tf), lambda i, j: (0, j)),
            pl.BlockSpec((d, tf), lambda i, j: (0, j)),
            pl.BlockSpec((tf, d), lambda i, j: (j, 0)),
            pl.BlockSpec((1, d), lambda i, j: (0, 0)),
        ],
        out_specs=pl.BlockSpec((tm, d), lambda i, j: (i, 0)),
        out_shape=jax.ShapeDtypeStruct((t, d), F32),
        scratch_shapes=[pltpu.VMEM((tm, d), BF16), pltpu.VMEM((tm, d), F32)],
        compiler_params=_params(2),
        name="ffn",
    )(x, gpre, wg, wu, wd, gpost)


def _inproj_body(x_ref, g_ref, w_ref, c_ref, sa_ref, sb_ref,
                 dq_ref, dk_ref, dvt_ref, sq_ref, sk_ref, svt_ref, mq_ref, mk_ref, mvt_ref,
                 gate_ref, kmean_ref):
    xn = _rms(x_ref[...], g_ref[...]).astype(BF16)
    cos, sin_a, sin_b = c_ref[...], sa_ref[...], sb_ref[...]
    scale = HEAD_DIM ** -0.5

    def proj(group):
        return _dot(xn, w_ref[:, group * GROUP_W:(group + 1) * GROUP_W])

    def rope(t):
        half = ROT_DIM // 2
        return (t * cos + pltpu.roll(t, LANES - half, 1) * sin_a
                + pltpu.roll(t, half, 1) * sin_b)

    def emit(out_ref, group, roped, scaled):
        t = proj(group)
        slabs = []
        for s in range(N_GROUPS):
            ts = t[:, s * LANES:(s + 1) * LANES]
            if roped:
                ts = rope(ts)
            slabs.append(ts)
            out_ref[:, s * LANES:(s + 1) * LANES] = (ts * scale if scaled else ts).astype(BF16)
        return slabs

    emit(dq_ref, 0, True, True)
    emit(dk_ref, 1, True, False)
    emit(sq_ref, 3, False, True)
    emit(sk_ref, 4, False, False)
    emit(mq_ref, 6, True, True)
    mk_slabs = emit(mk_ref, 7, True, False)
    for s, ts in enumerate(mk_slabs):
        kmean_ref[0, :, s * LANES:(s + 1) * LANES] = jnp.mean(ts, axis=0, keepdims=True)
    for group, vt_ref in zip(V_GROUPS, (dvt_ref, svt_ref, mvt_ref)):
        vt_ref[0] = lax.dot_general(w_ref[:, group * GROUP_W:(group + 1) * GROUP_W], xn, _TN,
                                    preferred_element_type=F32).astype(BF16)
    for gi in range(N_GATE_GROUPS):
        gate_ref[:, gi * GROUP_W:(gi + 1) * GROUP_W] = _sigmoid(proj(N_QKV_GROUPS + gi))


def _inproj(x, g, w, cos_t, sina_t, sinb_t, seq):
    t, d = x.shape
    tm = KEY_TILE
    n_pos_tiles = seq // tm
    qk_spec = pl.BlockSpec((tm, GROUP_W), lambda i: (i, 0))
    vt_spec = pl.BlockSpec((1, GROUP_W, tm), lambda i: (i, 0, 0))
    tab_spec = pl.BlockSpec((tm, LANES), lambda i: (i % n_pos_tiles, 0))
    qk_shape = jax.ShapeDtypeStruct((t, GROUP_W), BF16)
    vt_shape = jax.ShapeDtypeStruct((t // tm, GROUP_W, tm), BF16)
    gate_w = N_GATE_GROUPS * GROUP_W
    return pl.pallas_call(
        _inproj_body,
        grid=(t // tm,),
        in_specs=[
            pl.BlockSpec((tm, d), lambda i: (i, 0)),
            pl.BlockSpec((1, d), lambda i: (0, 0)),
            pl.BlockSpec(w.shape, lambda i: (0, 0), pipeline_mode=pl.Buffered(1)),
            tab_spec, tab_spec, tab_spec,
        ],
        out_specs=[qk_spec, qk_spec, vt_spec] * 3 + [
            pl.BlockSpec((tm, gate_w), lambda i: (i, 0)),
            pl.BlockSpec((1, 1, GROUP_W), lambda i: (i, 0, 0)),
        ],
        out_shape=[qk_shape, qk_shape, vt_shape] * 3 + [
            jax.ShapeDtypeStruct((t, gate_w), F32),
            jax.ShapeDtypeStruct((t // tm, 1, GROUP_W), F32),
        ],
        compiler_params=_params(1),
        name="inproj",
    )(x, g, w, cos_t, sina_t, sinb_t)


def _stack_heads(q):
    lane = lax.broadcasted_iota(jnp.int32, q.shape, 1)
    zero = jnp.zeros_like(q)
    return jnp.concatenate([jnp.where(lane < HEAD_DIM, q, zero),
                            jnp.where(lane >= HEAD_DIM, q, zero)], axis=0)


def _unstack_heads(o_t, tq):
    return jnp.concatenate([o_t[:HEAD_DIM, :tq], o_t[HEAD_DIM:, tq:]], axis=0).T


def _tile_positions(tk, tq, k0, q0):
    key = lax.broadcasted_iota(jnp.int32, (tk, 2 * tq), 0) + k0
    col = lax.broadcasted_iota(jnp.int32, (tk, 2 * tq), 1)
    query = jnp.where(col >= tq, col - tq, col) + q0
    return key, query


def _pv(vt_ref, j0, p, per_head):
    ones = jnp.ones((SUM_ROWS, KEY_TILE), BF16)
    half = p.shape[1] // 2
    out = None
    for n in range(p.shape[0] // KEY_TILE):
        vt = vt_ref[j0 + n]
        pn = p[n * KEY_TILE:(n + 1) * KEY_TILE]
        if per_head:
            part = jnp.concatenate(
                [_dot(jnp.concatenate([vt[:HEAD_DIM], ones], axis=0), pn[:, :half]),
                 _dot(jnp.concatenate([vt[HEAD_DIM:], ones], axis=0), pn[:, half:])], axis=1)
        else:
            part = _dot(jnp.concatenate([vt, ones], axis=0), pn)
        out = part if out is None else out + part
    return out


def _softmax_step(s, s_max, vt_ref, j0, m_ref, acc_ref, per_head):
    m_prev = m_ref[...]
    m_new = jnp.maximum(m_prev, s_max)
    alpha = jnp.exp(m_prev - m_new)
    acc_ref[...] = alpha * acc_ref[...] + _pv(vt_ref, j0, jnp.exp(s - m_new).astype(BF16), per_head)
    m_ref[...] = m_new


def _softmax_init(m_ref, acc_ref):
    m_ref[...] = jnp.full_like(m_ref, MASKED)
    acc_ref[...] = jnp.zeros_like(acc_ref)


def _softmax_result(acc_ref, n_features):
    return acc_ref[:n_features] / acc_ref[n_features:n_features + 1]


def _col_max(s):
    return jnp.max(s, axis=0, keepdims=True)


def _pipelined_steps(n_regular, logits, update, final, s_ref, cmax_ref):
    def ahead(n, buf):
        s = logits(n)
        s_ref[buf] = s
        cmax_ref[buf] = _col_max(s)

    def consume(n, buf):
        update(n, s_ref[buf], cmax_ref[buf])

    odd = n_regular % 2
    s0 = logits(0)
    s_ref[odd] = s0
    cmax_ref[odd] = _col_max(s0)

    @pl.when(odd == 1)
    def _():
        ahead(1, 0)
        consume(0, 1)

    def pair(nn, carry):
        t = odd + 2 * nn
        ahead(t + 1, 1)
        consume(t, 0)
        ahead(t + 2, 0)
        consume(t + 1, 1)
        return carry

    lax.fori_loop(0, n_regular // 2, pair, 0)
    final(s_ref[0])


def _step_buffers(tq):
    return [pltpu.VMEM((2, tq, 2 * tq), F32), pltpu.VMEM((2, 1, 2 * tq), F32)]


def _attn_call(body, name, q, k, vt, extra_inputs, extra_specs, scratch, *, batch, seq, tq):
    rows_per_batch = seq // tq
    n_key_tiles = seq // KEY_TILE
    vt = vt.reshape(batch, n_key_tiles, GROUP_W, KEY_TILE)
    q_spec = pl.BlockSpec((tq, LANES), lambda b, h, i: (b * rows_per_batch + i, h))
    k_spec = pl.BlockSpec((seq, LANES), lambda b, h, i: (b, h))
    vt_spec = pl.BlockSpec((None, n_key_tiles, LANES, KEY_TILE), lambda b, h, i: (b, 0, h, 0))
    return pl.pallas_call(
        body,
        grid=(batch, N_GROUPS, rows_per_batch),
        in_specs=extra_specs + [q_spec, k_spec, vt_spec],
        out_specs=q_spec,
        out_shape=jax.ShapeDtypeStruct(q.shape, BF16),
        scratch_shapes=scratch,
        compiler_params=_params(3),
        name=name,
    )(*extra_inputs, q, k, vt)


def _row_stat(tq):
    return pltpu.VMEM((1, 2 * tq), F32)


def _acc(tq, n_features=LANES):
    return pltpu.VMEM((n_features, 2 * tq), F32)


def _diff_body(lam_ref, g_ref, q_ref, k_ref, vt_ref, o_ref, m_ref, acc_ref, s_ref, cmax_ref,
               *, tq, post_scale):
    i = pl.program_id(2)
    qq = _stack_heads(q_ref[...])
    _softmax_init(m_ref, acc_ref)
    tiles = tq // KEY_TILE

    def logits(n):
        k0 = pl.multiple_of(n * tq, tq)
        return _dot_nt(k_ref[pl.ds(k0, tq), :], qq)

    def update(n, s, s_max):
        _softmax_step(s, s_max, vt_ref, n * tiles, m_ref, acc_ref, False)

    def final(s):
        key, query = _tile_positions(tq, tq, 0, 0)
        s = jnp.where(key <= query, s, MASKED)
        _softmax_step(s, _col_max(s), vt_ref, i * tiles, m_ref, acc_ref, False)

    _pipelined_steps(i, logits, update, final, s_ref, cmax_ref)

    o_t = _softmax_result(acc_ref, LANES)
    out = (o_t[:, :tq] - lam_ref[0, 0] * o_t[:, tq:]).T
    o_ref[...] = (_rms(out, g_ref[...]) * post_scale).astype(BF16)


def _diff_attention(q, k, vt, lam, gain, *, batch, seq, post_scale, tq=ATTN_TQ):
    body = functools.partial(_diff_body, tq=tq, post_scale=post_scale)
    return _attn_call(
        body, "diff_attn", q, k, vt,
        [lam, gain],
        [pl.BlockSpec(memory_space=pltpu.SMEM), pl.BlockSpec((1, LANES), lambda b, h, i: (0, 0))],
        [_row_stat(tq), _acc(tq, LANES + SUM_ROWS)] + _step_buffers(tq),
        batch=batch, seq=seq, tq=tq)


def _sb_body(q_ref, k_ref, vt_ref, o_ref, c_ref, acc_ref, *, tq, n_sub):
    tk = KEY_TILE
    row = lax.broadcasted_iota(jnp.int32, (tk + SUM_ROWS, tk), 0)
    later = jnp.logical_or(lax.broadcasted_iota(jnp.int32, row.shape, 1) > row, row >= tk).astype(BF16)
    key, query = _tile_positions(tk, tq, 0, 0)
    valid = key < query
    first = [_sb_first_tiles(pl.program_id(2) * n_sub + h, _stack_heads(q_ref[h * tq:(h + 1) * tq]),
                             k_ref, vt_ref, later, valid, c_ref.at[h], acc_ref.at[h])
             for h in range(n_sub)]
    for h, (i, qq, c_max) in enumerate(first):
        _sb_earlier_tiles(i, qq, c_max, k_ref, vt_ref, later, c_ref.at[h], acc_ref.at[h])
        o_ref[h * tq:(h + 1) * tq] = _unstack_heads(acc_ref[h], tq).astype(BF16)


def _sb_tile_terms(j, qq, k_ref, later, valid):
    tk = KEY_TILE
    k0 = pl.multiple_of(j * tk, tk)
    z = _dot_nt(k_ref[pl.ds(k0, tk), :], qq)
    softplus = jnp.maximum(z, 0.0) + jnp.log(1.0 + jnp.exp(-jnp.abs(z)))
    log_keep = -softplus
    if valid is not None:
        log_keep = jnp.where(valid, log_keep, 0.0)
    hi = log_keep.astype(BF16)
    lo = (log_keep - hi.astype(F32)).astype(BF16)
    sums = _dot(later, hi) + _dot(later, lo)
    return (z - softplus) + sums[:tk], sums[tk:tk + 1]


def _sb_first_tiles(i, qq, k_ref, vt_ref, later, valid, c_ref, acc_ref):
    log_w, total = _sb_tile_terms(i, qq, k_ref, later, valid)
    w = jnp.where(valid, jnp.exp(log_w), 0.0)
    has_prev = i > 0
    prev = jnp.maximum(i - 1, 0)
    log_w_prev, total_prev = _sb_tile_terms(prev, qq, k_ref, later, None)
    w_prev = jnp.exp(log_w_prev + total)
    vt_prev = vt_ref[prev]
    vt_prev = jnp.where(has_prev, vt_prev, jnp.zeros_like(vt_prev))
    acc_ref[...] = _dot(vt_ref[i], w.astype(BF16)) + _dot(vt_prev, w_prev.astype(BF16))
    c = total + jnp.where(has_prev, total_prev, 0.0)
    c_ref[...] = c
    return i, qq, jnp.max(c)


def _sb_earlier_tiles(i, qq, c_max, k_ref, vt_ref, later, c_ref, acc_ref):
    def cond(state):
        j, c_max = state
        return jnp.logical_and(j >= 0, c_max > SB_LOG_ZERO)

    def body(state):
        j, _ = state
        log_w, total = _sb_tile_terms(j, qq, k_ref, later, None)
        c = c_ref[...]
        acc_ref[...] += _dot(vt_ref[j], jnp.exp(log_w + c).astype(BF16))
        c_ref[...] = c + total
        return j - 1, jnp.max(c + total)

    lax.while_loop(cond, body, (i - 2, c_max))


def _sb_attention(q, k, vt, *, batch, seq, n_sub=2):
    tq = KEY_TILE
    body = functools.partial(_sb_body, tq=tq, n_sub=n_sub)
    return _attn_call(body, "sb_attn", q, k, vt, [], [],
                      [pltpu.VMEM((n_sub, 1, 2 * tq), F32), pltpu.VMEM((n_sub, LANES, 2 * tq), F32)],
                      batch=batch, seq=seq, tq=n_sub * tq)


def _moba_body(km_ref, q_ref, k_ref, vt_ref, o_ref, m_ref, acc_ref, s_ref, cmax_ref, *, tq):
    i = pl.program_id(2)
    tk = MOBA_BLOCK
    blocks = tq // tk
    qq = _stack_heads(q_ref[...])

    gate = _dot_nt(km_ref[0], qq)
    blk = lax.broadcasted_iota(jnp.int32, gate.shape, 0)
    col = lax.broadcasted_iota(jnp.int32, (1, 2 * tq), 1)
    own = i * blocks + jnp.where(col >= tq, col - tq, col) // tk
    gate = jnp.where(blk < own, gate, -jnp.inf)
    selected = blk == own
    for _ in range(MOBA_TOPK):
        best = jnp.max(gate, axis=0, keepdims=True)
        idx = jnp.min(jnp.where(gate == best, blk, LANES), axis=0, keepdims=True)
        pick = jnp.logical_and(blk == idx, best > -jnp.inf)
        selected = jnp.logical_or(selected, pick)
        gate = jnp.where(blk == idx, -jnp.inf, gate)
    bias = jnp.where(selected, 0.0, MASKED)
    bias = jnp.concatenate([bias, jnp.zeros((LANES - bias.shape[0], 2 * tq), F32)], axis=0)
    qq = jnp.concatenate([qq, bias.T.astype(BF16)], axis=1)
    key_blk = lax.broadcasted_iota(jnp.int32, (tq, LANES), 0) // tk
    lane = lax.broadcasted_iota(jnp.int32, (tq, LANES), 1)

    _softmax_init(m_ref, acc_ref)

    def logits(n):
        k0 = pl.multiple_of(n * tq, tq)
        one_hot = (lane == key_blk + n * blocks).astype(BF16)
        return _dot_nt(jnp.concatenate([k_ref[pl.ds(k0, tq), :], one_hot], axis=1), qq)

    def update(n, s, s_max):
        _softmax_step(s, s_max, vt_ref, n * blocks, m_ref, acc_ref, True)

    def final(s):
        key, query = _tile_positions(tq, tq, 0, 0)
        s = jnp.where(key <= query, s, MASKED)
        _softmax_step(s, _col_max(s), vt_ref, i * blocks, m_ref, acc_ref, True)

    _pipelined_steps(i, logits, update, final, s_ref, cmax_ref)
    o_t = _softmax_result(acc_ref, HEAD_DIM)
    o_ref[...] = jnp.concatenate([o_t[:, :tq], o_t[:, tq:]], axis=0).T.astype(BF16)


def _moba_attention(q, k, vt, kmean, *, batch, seq, tq=ATTN_TQ):
    body = functools.partial(_moba_body, tq=tq)
    return _attn_call(
        body, "moba_attn", q, k, vt,
        [kmean],
        [pl.BlockSpec((1, kmean.shape[1], LANES), lambda b, h, i: (b, 0, h))],
        [_row_stat(tq), _acc(tq, HEAD_DIM + SUM_ROWS)] + _step_buffers(tq),
        batch=batch, seq=seq, tq=tq)


def _merge_body(x_ref, a_ref, b_ref, c_ref, gate_ref, wa_ref, wb_ref, wc_ref, wo_ref, g_ref, o_ref):
    d = x_ref.shape[1]
    merged = (gate_ref[:, 0:d] * _dot(a_ref[...], wa_ref[...])
              + gate_ref[:, d:2 * d] * _dot(b_ref[...], wb_ref[...])
              + gate_ref[:, 2 * d:3 * d] * _dot(c_ref[...], wc_ref[...]))
    y = _dot(merged.astype(BF16), wo_ref[...])
    o_ref[...] = x_ref[...] + _rms(y, g_ref[...])


def _merge(x, a, b, c, gates, wa, wb, wc, wo, g, *, tm=512):
    t, d = x.shape
    row = lambda w: pl.BlockSpec((tm, w), lambda i: (i, 0))
    whole = lambda arr: pl.BlockSpec(arr.shape, lambda i: (0, 0))
    return pl.pallas_call(
        _merge_body,
        grid=(t // tm,),
        in_specs=[row(d), row(GROUP_W), row(GROUP_W), row(GROUP_W), row(3 * d),
                  whole(wa), whole(wb), whole(wc), whole(wo), whole(g)],
        out_specs=row(d),
        out_shape=jax.ShapeDtypeStruct((t, d), F32),
        compiler_params=_params(1),
        name="merge",
    )(x, a, b, c, gates, wa, wb, wc, wo, g)


def _rope_lane_tables(seq):
    pos = jnp.arange(seq, dtype=F32)
    inv_freq = ROPE_THETA ** (-jnp.arange(0, ROT_DIM, 2, dtype=F32) / ROT_DIM)
    ang = pos[:, None] * inv_freq[None, :]
    cos, sin = jnp.cos(ang), jnp.sin(ang)
    half = ROT_DIM // 2
    rest = HEAD_DIM - ROT_DIM
    cos_h = jnp.concatenate([cos, cos, jnp.ones((seq, rest), F32)], axis=1)
    sina_h = jnp.concatenate([-sin, jnp.zeros((seq, half + rest), F32)], axis=1)
    sinb_h = jnp.concatenate([jnp.zeros((seq, half), F32), sin, jnp.zeros((seq, rest), F32)], axis=1)
    reps = LANES // HEAD_DIM
    return tuple(jnp.tile(t, (1, reps)) for t in (cos_h, sina_h, sinb_h))


def kernel(x, w_in, w_diff_o, w_sb_o, w_moba_o, w_out, lam_q1, lam_k1, lam_q2, lam_k2,
           diff_norm_g, ffn1_wg, ffn1_wu, ffn1_wd, ffn2_wg, ffn2_wu, ffn2_wd,
           g_ffn1_pre, g_ffn1_post, g_mix_pre, g_mix_post, g_ffn2_pre, g_ffn2_post):
    batch, seq, d = x.shape
    depth = w_in.shape[0]
    n_blocks = seq // MOBA_BLOCK
    assert seq % 512 == 0 and n_blocks <= LANES
    cos_t, sina_t, sinb_t = _rope_lane_tables(seq)
    bf = lambda w: w.astype(BF16)
    vec = lambda g: g.reshape(1, -1)
    xt = x.reshape(batch * seq, d)
    attn = dict(batch=batch, seq=seq)

    for l in range(depth):
        lambda_init = 0.8 - 0.6 * math.exp(-0.3 * l)
        xt = _ffn(xt, vec(g_ffn1_pre[l]), bf(ffn1_wg[l]), bf(ffn1_wu[l]), bf(ffn1_wd[l]),
                  vec(g_ffn1_post[l]))

        (dq, dk, dvt, sq, sk, svt, mq, mk, mvt, gates, kmean) = _inproj(
            xt, vec(g_mix_pre[l]), bf(w_in[l]), cos_t, sina_t, sinb_t, seq)
        lam = (jnp.exp(jnp.sum(lam_q1[l] * lam_k1[l])) - jnp.exp(jnp.sum(lam_q2[l] * lam_k2[l]))
               + lambda_init).reshape(1, 1)
        a_out = _diff_attention(dq, dk, dvt, lam, vec(diff_norm_g[l]),
                                post_scale=1.0 - lambda_init, **attn)
        b_out = _sb_attention(sq, sk, svt, **attn)
        kmean = kmean.reshape(batch, n_blocks, GROUP_W)
        kmean = jnp.pad(kmean, ((0, 0), (0, -n_blocks % BF16_ROWS), (0, 0))).astype(BF16)
        c_out = _moba_attention(mq, mk, mvt, kmean, **attn)

        xt = _merge(xt, a_out, b_out, c_out, gates, bf(w_diff_o[l]), bf(w_sb_o[l]), bf(w_moba_o[l]),
                    bf(w_out[l]), vec(g_mix_post[l]))

        xt = _ffn(xt, vec(g_ffn2_pre[l]), bf(ffn2_wg[l]), bf(ffn2_wu[l]), bf(ffn2_wd[l]),
                  vec(g_ffn2_post[l]))
    return xt.reshape(batch, seq, d)
```

```python
import functools
import math

import jax
import jax.numpy as jnp
from jax import lax
from jax.experimental import pallas as pl
from jax.experimental.pallas import tpu as pltpu

F32 = jnp.float32
BF16 = jnp.bfloat16

LANES = 128
HEAD_DIM = 64
ROT_DIM = HEAD_DIM // 4
ROPE_THETA = 500000.0
MOBA_BLOCK = 256
MOBA_TOPK = 3
NORM_EPS = 1e-6
GROUP_W = 512
N_GROUPS = GROUP_W // LANES
N_QKV_GROUPS = 9
N_GATE_GROUPS = 6
V_GROUPS = (2, 5, 8)
KEY_TILE = MOBA_BLOCK
SB_LOG_ZERO = -104.0
MASKED = -1e30
BF16_ROWS = 16
SUM_ROWS = BF16_ROWS
ATTN_TQ = 512
VMEM_LIMIT = 56 * 1024 * 1024

_NT = (((1,), (1,)), ((), ()))
_TN = (((0,), (1,)), ((), ()))


def _dot(a, b):
    return jnp.dot(a, b, preferred_element_type=F32)


def _dot_nt(a, b):
    return lax.dot_general(a, b, _NT, preferred_element_type=F32)


def _sigmoid(x):
    return 1.0 / (1.0 + jnp.exp(-x))


def _rms(x, gain):
    return x * lax.rsqrt(jnp.mean(x * x, axis=-1, keepdims=True) + NORM_EPS) * gain


def _params(n_axes):
    return pltpu.CompilerParams(dimension_semantics=("arbitrary",) * n_axes,
                                vmem_limit_bytes=VMEM_LIMIT)


def _ffn_body(x_ref, gpre_ref, wg_ref, wu_ref, wd_ref, gpost_ref, o_ref, xn_ref, acc_ref):
    f = pl.program_id(1)

    @pl.when(f == 0)
    def _():
        xn_ref[...] = _rms(x_ref[...], gpre_ref[...]).astype(BF16)
        acc_ref[...] = jnp.zeros_like(acc_ref)

    xn = xn_ref[...]
    g = _dot(xn, wg_ref[...])
    u = _dot(xn, wu_ref[...])
    a = (g * _sigmoid(g)) * u
    acc_ref[...] += _dot(a.astype(BF16), wd_ref[...])

    @pl.when(f == pl.num_programs(1) - 1)
    def _():
        o_ref[...] = x_ref[...] + 0.5 * _rms(acc_ref[...], gpost_ref[...])


def _ffn(x, gpre, wg, wu, wd, gpost, *, tm=512, tf=1408):
    t, d = x.shape
    f = wg.shape[1]
    return pl.pallas_call(
        _ffn_body,
        grid=(t // tm, f // tf),
        in_specs=[
            pl.BlockSpec((tm, d), lambda i, j: (i, 0)),
            pl.BlockSpec((1, d), lambda i, j: (0, 0)),
            pl.BlockSpec((d, tf), lambda i, j: (0, j)),
            pl.BlockSpec((d, tf), lambda i, j: (0, j)),
            pl.BlockSpec((tf, d), lambda i, j: (j, 0)),
            pl.BlockSpec((1, d), lambda i, j: (0, 0)),
        ],
        out_specs=pl.BlockSpec((tm, d), lambda i, j: (i, 0)),
        out_shape=jax.ShapeDtypeStruct((t, d), F32),
        scratch_shapes=[pltpu.VMEM((tm, d), BF16), pltpu.VMEM((tm, d), F32)],
        compiler_params=_params(2),
        name="ffn",
    )(x, gpre, wg, wu, wd, gpost)


def _inproj_body(x_ref, g_ref, w_ref, c_ref, sa_ref, sb_ref,
                 dq_ref, dk_ref, dvt_ref, sq_ref, sk_ref, svt_ref, mq_ref, mk_ref, mvt_ref,
                 gate_ref, kmean_ref):
    xn = _rms(x_ref[...], g_ref[...]).astype(BF16)
    cos, sin_a, sin_b = c_ref[...], sa_ref[...], sb_ref[...]
    scale = HEAD_DIM ** -0.5

    def proj(group):
        return _dot(xn, w_ref[:, group * GROUP_W:(group + 1) * GROUP_W])

    def rope(t):
        half = ROT_DIM // 2
        return (t * cos + pltpu.roll(t, LANES - half, 1) * sin_a
                + pltpu.roll(t, half, 1) * sin_b)

    def emit(out_ref, group, roped, scaled):
        t = proj(group)
        slabs = []
        for s in range(N_GROUPS):
            ts = t[:, s * LANES:(s + 1) * LANES]
            if roped:
                ts = rope(ts)
            slabs.append(ts)
            out_ref[:, s * LANES:(s + 1) * LANES] = (ts * scale if scaled else ts).astype(BF16)
        return slabs

    emit(dq_ref, 0, True, True)
    emit(dk_ref, 1, True, False)
    emit(sq_ref, 3, False, True)
    emit(sk_ref, 4, False, False)
    emit(mq_ref, 6, True, True)
    mk_slabs = emit(mk_ref, 7, True, False)
    for s, ts in enumerate(mk_slabs):
        kmean_ref[0, :, s * LANES:(s + 1) * LANES] = jnp.mean(ts, axis=0, keepdims=True)
    for group, vt_ref in zip(V_GROUPS, (dvt_ref, svt_ref, mvt_ref)):
        vt_ref[0] = lax.dot_general(w_ref[:, group * GROUP_W:(group + 1) * GROUP_W], xn, _TN,
                                    preferred_element_type=F32).astype(BF16)
    for gi in range(N_GATE_GROUPS):
        gate_ref[:, gi * GROUP_W:(gi + 1) * GROUP_W] = _sigmoid(proj(N_QKV_GROUPS + gi))


def _inproj(x, g, w, cos_t, sina_t, sinb_t, seq):
    t, d = x.shape
    tm = KEY_TILE
    n_pos_tiles = seq // tm
    qk_spec = pl.BlockSpec((tm, GROUP_W), lambda i: (i, 0))
    vt_spec = pl.BlockSpec((1, GROUP_W, tm), lambda i: (i, 0, 0))
    tab_spec = pl.BlockSpec((tm, LANES), lambda i: (i % n_pos_tiles, 0))
    qk_shape = jax.ShapeDtypeStruct((t, GROUP_W), BF16)
    vt_shape = jax.ShapeDtypeStruct((t // tm, GROUP_W, tm), BF16)
    gate_w = N_GATE_GROUPS * GROUP_W
    return pl.pallas_call(
        _inproj_body,
        grid=(t // tm,),
        in_specs=[
            pl.BlockSpec((tm, d), lambda i: (i, 0)),
            pl.BlockSpec((1, d), lambda i: (0, 0)),
            pl.BlockSpec(w.shape, lambda i: (0, 0), pipeline_mode=pl.Buffered(1)),
            tab_spec, tab_spec, tab_spec,
        ],
        out_specs=[qk_spec, qk_spec, vt_spec] * 3 + [
            pl.BlockSpec((tm, gate_w), lambda i: (i, 0)),
            pl.BlockSpec((1, 1, GROUP_W), lambda i: (i, 0, 0)),
        ],
        out_shape=[qk_shape, qk_shape, vt_shape] * 3 + [
            jax.ShapeDtypeStruct((t, gate_w), F32),
            jax.ShapeDtypeStruct((t // tm, 1, GROUP_W), F32),
        ],
        compiler_params=_params(1),
        name="inproj",
    )(x, g, w, cos_t, sina_t, sinb_t)


def _stack_heads(q):
    lane = lax.broadcasted_iota(jnp.int32, q.shape, 1)
    zero = jnp.zeros_like(q)
    return jnp.concatenate([jnp.where(lane < HEAD_DIM, q, zero),
                            jnp.where(lane >= HEAD_DIM, q, zero)], axis=0)


def _unstack_heads(o_t, tq):
    return jnp.concatenate([o_t[:HEAD_DIM, :tq], o_t[HEAD_DIM:, tq:]], axis=0).T


def _tile_positions(tk, tq, k0, q0):
    key = lax.broadcasted_iota(jnp.int32, (tk, 2 * tq), 0) + k0
    col = lax.broadcasted_iota(jnp.int32, (tk, 2 * tq), 1)
    query = jnp.where(col >= tq, col - tq, col) + q0
    return key, query


def _pv(vt_ref, j0, p, per_head):
    ones = jnp.ones((SUM_ROWS, KEY_TILE), BF16)
    half = p.shape[1] // 2
    out = None
    for n in range(p.shape[0] // KEY_TILE):
        vt = vt_ref[j0 + n]
        pn = p[n * KEY_TILE:(n + 1) * KEY_TILE]
        if per_head:
            part = jnp.concatenate(
                [_dot(jnp.concatenate([vt[:HEAD_DIM], ones], axis=0), pn[:, :half]),
                 _dot(jnp.concatenate([vt[HEAD_DIM:], ones], axis=0), pn[:, half:])], axis=1)
        else:
            part = _dot(jnp.concatenate([vt, ones], axis=0), pn)
        out = part if out is None else out + part
    return out


def _softmax_step(s, s_max, vt_ref, j0, m_ref, acc_ref, per_head):
    m_prev = m_ref[...]
    m_new = jnp.maximum(m_prev, s_max)
    alpha = jnp.exp(m_prev - m_new)
    acc_ref[...] = alpha * acc_ref[...] + _pv(vt_ref, j0, jnp.exp(s - m_new).astype(BF16), per_head)
    m_ref[...] = m_new


def _softmax_init(m_ref, acc_ref):
    m_ref[...] = jnp.full_like(m_ref, MASKED)
    acc_ref[...] = jnp.zeros_like(acc_ref)


def _softmax_result(acc_ref, n_features):
    return acc_ref[:n_features] / acc_ref[n_features:n_features + 1]


def _col_max(s):
    return jnp.max(s, axis=0, keepdims=True)


def _pipelined_steps(n_regular, streams):
    def ahead(n, buf):
        for logits, _, _, s_ref, cmax_ref in streams:
            s = logits(n)
            s_ref[buf] = s
            cmax_ref[buf] = _col_max(s)

    def consume(n, buf):
        for _, update, _, s_ref, cmax_ref in streams:
            update(n, s_ref[buf], cmax_ref[buf])

    odd = n_regular % 2
    ahead(0, odd)

    @pl.when(odd == 1)
    def _():
        ahead(1, 0)
        consume(0, 1)

    def pair(nn, carry):
        t = odd + 2 * nn
        ahead(t + 1, 1)
        consume(t, 0)
        ahead(t + 2, 0)
        consume(t + 1, 1)
        return carry

    lax.fori_loop(0, n_regular // 2, pair, 0)
    for _, _, final, s_ref, _ in streams:
        final(s_ref[0])


def _step_buffers(tq):
    return [pltpu.VMEM((2, tq, 2 * tq), F32), pltpu.VMEM((2, 1, 2 * tq), F32)]


def _attn_call(body, name, qkv, extra_inputs, extra_specs, scratch, *, batch, seq, tq):
    rows_per_batch = seq // tq
    n_key_tiles = seq // KEY_TILE
    q_spec = pl.BlockSpec((tq, LANES), lambda b, h, i: (b * rows_per_batch + i, h))
    k_spec = pl.BlockSpec((seq, LANES), lambda b, h, i: (b, h))
    vt_spec = pl.BlockSpec((None, n_key_tiles, LANES, KEY_TILE), lambda b, h, i: (b, 0, h, 0))
    operands = []
    for q, k, vt in qkv:
        operands += [q, k, vt.reshape(batch, n_key_tiles, GROUP_W, KEY_TILE)]
    out_shape = jax.ShapeDtypeStruct(qkv[0][0].shape, BF16)
    outs = pl.pallas_call(
        body,
        grid=(batch, N_GROUPS, rows_per_batch),
        in_specs=extra_specs + [q_spec, k_spec, vt_spec] * len(qkv),
        out_specs=[q_spec] * len(qkv),
        out_shape=[out_shape] * len(qkv),
        scratch_shapes=scratch,
        compiler_params=_params(3),
        name=name,
    )(*extra_inputs, *operands)
    return outs if len(qkv) > 1 else outs[0]


def _row_stat(tq):
    return pltpu.VMEM((1, 2 * tq), F32)


def _acc(tq, n_features=LANES):
    return pltpu.VMEM((n_features, 2 * tq), F32)


def _diff_stream(q_ref, k_ref, vt_ref, m_ref, acc_ref, s_ref, cmax_ref, *, tq):
    i = pl.program_id(2)
    qq = _stack_heads(q_ref[...])
    _softmax_init(m_ref, acc_ref)
    tiles = tq // KEY_TILE

    def logits(n):
        k0 = pl.multiple_of(n * tq, tq)
        return _dot_nt(k_ref[pl.ds(k0, tq), :], qq)

    def update(n, s, s_max):
        _softmax_step(s, s_max, vt_ref, n * tiles, m_ref, acc_ref, False)

    def final(s):
        key, query = _tile_positions(tq, tq, 0, 0)
        s = jnp.where(key <= query, s, MASKED)
        _softmax_step(s, _col_max(s), vt_ref, i * tiles, m_ref, acc_ref, False)

    return logits, update, final, s_ref, cmax_ref


def _diff_finish(lam_ref, g_ref, o_ref, acc_ref, *, tq, post_scale):
    o_t = _softmax_result(acc_ref, LANES)
    out = (o_t[:, :tq] - lam_ref[0, 0] * o_t[:, tq:]).T
    o_ref[...] = (_rms(out, g_ref[...]) * post_scale).astype(BF16)


def _mixers_body(lam_ref, g_ref, km_ref, dq_ref, dk_ref, dvt_ref, sq_ref, sk_ref, svt_ref,
                 mq_ref, mk_ref, mvt_ref, a_ref, b_ref, c_ref,
                 dm_ref, dacc_ref, ds_ref, dcmax_ref, mm_ref, macc_ref, ms_ref, mcmax_ref,
                 sbc_ref, sbacc_ref, *, tq, post_scale):
    diff = _diff_stream(dq_ref, dk_ref, dvt_ref, dm_ref, dacc_ref, ds_ref, dcmax_ref, tq=tq)
    moba = _moba_stream(km_ref, mq_ref, mk_ref, mvt_ref, mm_ref, macc_ref, ms_ref, mcmax_ref, tq=tq)
    _pipelined_steps(pl.program_id(2), [diff, moba])
    _diff_finish(lam_ref, g_ref, a_ref, dacc_ref, tq=tq, post_scale=post_scale)
    _moba_finish(c_ref, macc_ref, tq=tq)
    _sb_body(sq_ref, sk_ref, svt_ref, b_ref, sbc_ref, sbacc_ref, tq=KEY_TILE, n_sub=tq // KEY_TILE)


def _mixers_attention(diff_qkv, sb_qkv, moba_qkv, lam, gain, kmean, *, batch, seq, post_scale,
                      tq=ATTN_TQ):
    body = functools.partial(_mixers_body, tq=tq, post_scale=post_scale)
    n_sub = tq // KEY_TILE
    return _attn_call(
        body, "mixers_attn", [diff_qkv, sb_qkv, moba_qkv],
        [lam, gain, kmean],
        [pl.BlockSpec(memory_space=pltpu.SMEM), pl.BlockSpec((1, LANES), lambda b, h, i: (0, 0)),
         pl.BlockSpec((1, kmean.shape[1], LANES), lambda b, h, i: (b, 0, h))],
        [_row_stat(tq), _acc(tq, LANES + SUM_ROWS)] + _step_buffers(tq)
        + [_row_stat(tq), _acc(tq, HEAD_DIM + SUM_ROWS)] + _step_buffers(tq)
        + [pltpu.VMEM((n_sub, 1, 2 * KEY_TILE), F32), pltpu.VMEM((n_sub, LANES, 2 * KEY_TILE), F32)],
        batch=batch, seq=seq, tq=tq)


def _sb_body(q_ref, k_ref, vt_ref, o_ref, c_ref, acc_ref, *, tq, n_sub):
    tk = KEY_TILE
    row = lax.broadcasted_iota(jnp.int32, (tk + SUM_ROWS, tk), 0)
    later = jnp.logical_or(lax.broadcasted_iota(jnp.int32, row.shape, 1) > row, row >= tk).astype(BF16)
    key, query = _tile_positions(tk, tq, 0, 0)
    valid = key < query
    first = [_sb_first_tiles(pl.program_id(2) * n_sub + h, _stack_heads(q_ref[h * tq:(h + 1) * tq]),
                             k_ref, vt_ref, later, valid, c_ref.at[h], acc_ref.at[h])
             for h in range(n_sub)]
    for h, (i, qq, c_max) in enumerate(first):
        _sb_earlier_tiles(i, qq, c_max, k_ref, vt_ref, later, c_ref.at[h], acc_ref.at[h])
        o_ref[h * tq:(h + 1) * tq] = _unstack_heads(acc_ref[h], tq).astype(BF16)


def _sb_tile_terms(j, qq, k_ref, later, valid):
    tk = KEY_TILE
    k0 = pl.multiple_of(j * tk, tk)
    z = _dot_nt(k_ref[pl.ds(k0, tk), :], qq)
    softplus = jnp.maximum(z, 0.0) + jnp.log(1.0 + jnp.exp(-jnp.abs(z)))
    log_keep = -softplus
    if valid is not None:
        log_keep = jnp.where(valid, log_keep, 0.0)
    hi = log_keep.astype(BF16)
    lo = (log_keep - hi.astype(F32)).astype(BF16)
    sums = _dot(later, hi) + _dot(later, lo)
    return (z - softplus) + sums[:tk], sums[tk:tk + 1]


def _sb_first_tiles(i, qq, k_ref, vt_ref, later, valid, c_ref, acc_ref):
    log_w, total = _sb_tile_terms(i, qq, k_ref, later, valid)
    w = jnp.where(valid, jnp.exp(log_w), 0.0)
    has_prev = i > 0
    prev = jnp.maximum(i - 1, 0)
    log_w_prev, total_prev = _sb_tile_terms(prev, qq, k_ref, later, None)
    w_prev = jnp.exp(log_w_prev + total)
    vt_prev = vt_ref[prev]
    vt_prev = jnp.where(has_prev, vt_prev, jnp.zeros_like(vt_prev))
    acc_ref[...] = _dot(vt_ref[i], w.astype(BF16)) + _dot(vt_prev, w_prev.astype(BF16))
    c = total + jnp.where(has_prev, total_prev, 0.0)
    c_ref[...] = c
    return i, qq, jnp.max(c)


def _sb_earlier_tiles(i, qq, c_max, k_ref, vt_ref, later, c_ref, acc_ref):
    def cond(state):
        j, c_max = state
        return jnp.logical_and(j >= 0, c_max > SB_LOG_ZERO)

    def body(state):
        j, _ = state
        log_w, total = _sb_tile_terms(j, qq, k_ref, later, None)
        c = c_ref[...]
        acc_ref[...] += _dot(vt_ref[j], jnp.exp(log_w + c).astype(BF16))
        c_ref[...] = c + total
        return j - 1, jnp.max(c + total)

    lax.while_loop(cond, body, (i - 2, c_max))


def _moba_stream(km_ref, q_ref, k_ref, vt_ref, m_ref, acc_ref, s_ref, cmax_ref, *, tq):
    i = pl.program_id(2)
    tk = MOBA_BLOCK
    blocks = tq // tk
    qq = _stack_heads(q_ref[...])

    gate = _dot_nt(km_ref[0], qq)
    blk = lax.broadcasted_iota(jnp.int32, gate.shape, 0)
    col = lax.broadcasted_iota(jnp.int32, (1, 2 * tq), 1)
    own = i * blocks + jnp.where(col >= tq, col - tq, col) // tk
    gate = jnp.where(blk < own, gate, -jnp.inf)
    selected = blk == own
    for _ in range(MOBA_TOPK):
        best = jnp.max(gate, axis=0, keepdims=True)
        idx = jnp.min(jnp.where(gate == best, blk, LANES), axis=0, keepdims=True)
        pick = jnp.logical_and(blk == idx, best > -jnp.inf)
        selected = jnp.logical_or(selected, pick)
        gate = jnp.where(blk == idx, -jnp.inf, gate)
    bias = jnp.where(selected, 0.0, MASKED)
    bias = jnp.concatenate([bias, jnp.zeros((LANES - bias.shape[0], 2 * tq), F32)], axis=0)
    qq = jnp.concatenate([qq, bias.T.astype(BF16)], axis=1)
    key_blk = lax.broadcasted_iota(jnp.int32, (tq, LANES), 0) // tk
    lane = lax.broadcasted_iota(jnp.int32, (tq, LANES), 1)

    _softmax_init(m_ref, acc_ref)

    def logits(n):
        k0 = pl.multiple_of(n * tq, tq)
        one_hot = (lane == key_blk + n * blocks).astype(BF16)
        return _dot_nt(jnp.concatenate([k_ref[pl.ds(k0, tq), :], one_hot], axis=1), qq)

    def update(n, s, s_max):
        _softmax_step(s, s_max, vt_ref, n * blocks, m_ref, acc_ref, True)

    def final(s):
        key, query = _tile_positions(tq, tq, 0, 0)
        s = jnp.where(key <= query, s, MASKED)
        _softmax_step(s, _col_max(s), vt_ref, i * blocks, m_ref, acc_ref, True)

    return logits, update, final, s_ref, cmax_ref


def _moba_finish(o_ref, acc_ref, *, tq):
    o_t = _softmax_result(acc_ref, HEAD_DIM)
    o_ref[...] = jnp.concatenate([o_t[:, :tq], o_t[:, tq:]], axis=0).T.astype(BF16)


def _merge_body(x_ref, a_ref, b_ref, c_ref, gate_ref, wa_ref, wb_ref, wc_ref, wo_ref, g_ref, o_ref):
    d = x_ref.shape[1]
    merged = (gate_ref[:, 0:d] * _dot(a_ref[...], wa_ref[...])
              + gate_ref[:, d:2 * d] * _dot(b_ref[...], wb_ref[...])
              + gate_ref[:, 2 * d:3 * d] * _dot(c_ref[...], wc_ref[...]))
    y = _dot(merged.astype(BF16), wo_ref[...])
    o_ref[...] = x_ref[...] + _rms(y, g_ref[...])


def _merge(x, a, b, c, gates, wa, wb, wc, wo, g, *, tm=512):
    t, d = x.shape
    row = lambda w: pl.BlockSpec((tm, w), lambda i: (i, 0))
    whole = lambda arr: pl.BlockSpec(arr.shape, lambda i: (0, 0))
    return pl.pallas_call(
        _merge_body,
        grid=(t // tm,),
        in_specs=[row(d), row(GROUP_W), row(GROUP_W), row(GROUP_W), row(3 * d),
                  whole(wa), whole(wb), whole(wc), whole(wo), whole(g)],
        out_specs=row(d),
        out_shape=jax.ShapeDtypeStruct((t, d), F32),
        compiler_params=_params(1),
        name="merge",
    )(x, a, b, c, gates, wa, wb, wc, wo, g)


def _rope_lane_tables(seq):
    pos = jnp.arange(seq, dtype=F32)
    inv_freq = ROPE_THETA ** (-jnp.arange(0, ROT_DIM, 2, dtype=F32) / ROT_DIM)
    ang = pos[:, None] * inv_freq[None, :]
    cos, sin = jnp.cos(ang), jnp.sin(ang)
    half = ROT_DIM // 2
    rest = HEAD_DIM - ROT_DIM
    cos_h = jnp.concatenate([cos, cos, jnp.ones((seq, rest), F32)], axis=1)
    sina_h = jnp.concatenate([-sin, jnp.zeros((seq, half + rest), F32)], axis=1)
    sinb_h = jnp.concatenate([jnp.zeros((seq, half), F32), sin, jnp.zeros((seq, rest), F32)], axis=1)
    reps = LANES // HEAD_DIM
    return tuple(jnp.tile(t, (1, reps)) for t in (cos_h, sina_h, sinb_h))


def kernel(x, w_in, w_diff_o, w_sb_o, w_moba_o, w_out, lam_q1, lam_k1, lam_q2, lam_k2,
           diff_norm_g, ffn1_wg, ffn1_wu, ffn1_wd, ffn2_wg, ffn2_wu, ffn2_wd,
           g_ffn1_pre, g_ffn1_post, g_mix_pre, g_mix_post, g_ffn2_pre, g_ffn2_post):
    batch, seq, d = x.shape
    depth = w_in.shape[0]
    n_blocks = seq // MOBA_BLOCK
    assert seq % 512 == 0 and n_blocks <= LANES
    cos_t, sina_t, sinb_t = _rope_lane_tables(seq)
    bf = lambda w: w.astype(BF16)
    vec = lambda g: g.reshape(1, -1)
    xt = x.reshape(batch * seq, d)
    attn = dict(batch=batch, seq=seq)

    for l in range(depth):
        lambda_init = 0.8 - 0.6 * math.exp(-0.3 * l)
        xt = _ffn(xt, vec(g_ffn1_pre[l]), bf(ffn1_wg[l]), bf(ffn1_wu[l]), bf(ffn1_wd[l]),
                  vec(g_ffn1_post[l]))

        (dq, dk, dvt, sq, sk, svt, mq, mk, mvt, gates, kmean) = _inproj(
            xt, vec(g_mix_pre[l]), bf(w_in[l]), cos_t, sina_t, sinb_t, seq)
        lam = (jnp.exp(jnp.sum(lam_q1[l] * lam_k1[l])) - jnp.exp(jnp.sum(lam_q2[l] * lam_k2[l]))
               + lambda_init).reshape(1, 1)
        kmean = kmean.reshape(batch, n_blocks, GROUP_W)
        kmean = jnp.pad(kmean, ((0, 0), (0, -n_blocks % BF16_ROWS), (0, 0))).astype(BF16)
        a_out, b_out, c_out = _mixers_attention(
            (dq, dk, dvt), (sq, sk, svt), (mq, mk, mvt), lam, vec(diff_norm_g[l]), kmean,
            post_scale=1.0 - lambda_init, **attn)

        xt = _merge(xt, a_out, b_out, c_out, gates, bf(w_diff_o[l]), bf(w_sb_o[l]), bf(w_moba_o[l]),
                    bf(w_out[l]), vec(g_mix_post[l]))

        xt = _ffn(xt, vec(g_ffn2_pre[l]), bf(ffn2_wg[l]), bf(ffn2_wu[l]), bf(ffn2_wd[l]),
                  vec(g_ffn2_post[l]))
    return xt.reshape(batch, seq, d)
```

```python
import functools
import math

import jax
import jax.numpy as jnp
from jax import lax
from jax.experimental import pallas as pl
from jax.experimental.pallas import tpu as pltpu

F32 = jnp.float32
BF16 = jnp.bfloat16

LANES = 128
HEAD_DIM = 64
ROT_DIM = HEAD_DIM // 4
ROPE_THETA = 500000.0
MOBA_BLOCK = 256
MOBA_TOPK = 3
NORM_EPS = 1e-6
GROUP_W = 512
N_GROUPS = GROUP_W // LANES
N_QKV_GROUPS = 9
N_GATE_GROUPS = 6
V_GROUPS = (2, 5, 8)
KEY_TILE = MOBA_BLOCK
SB_LOG_ZERO = -104.0
LOG2_E = math.log2(math.e)
MASKED = -1e30
BF16_ROWS = 16
SUM_ROWS = BF16_ROWS
ATTN_TQ = 512
VMEM_LIMIT = 56 * 1024 * 1024

_NT = (((1,), (1,)), ((), ()))
_TN = (((0,), (1,)), ((), ()))


def _dot(a, b):
    return jnp.dot(a, b, preferred_element_type=F32)


def _dot_nt(a, b):
    return lax.dot_general(a, b, _NT, preferred_element_type=F32)


def _sigmoid(x):
    return 1.0 / (1.0 + jnp.exp(-x))


def _rms(x, gain):
    return x * lax.rsqrt(jnp.mean(x * x, axis=-1, keepdims=True) + NORM_EPS) * gain


def _params(n_axes):
    return pltpu.CompilerParams(dimension_semantics=("arbitrary",) * n_axes,
                                vmem_limit_bytes=VMEM_LIMIT)


def _ffn_body(x_ref, gpre_ref, wg_ref, wu_ref, wd_ref, gpost_ref, o_ref, xn_ref, acc_ref):
    f = pl.program_id(1)

    @pl.when(f == 0)
    def _():
        xn_ref[...] = _rms(x_ref[...], gpre_ref[...]).astype(BF16)
        acc_ref[...] = jnp.zeros_like(acc_ref)

    xn = xn_ref[...]
    g = _dot(xn, wg_ref[...])
    u = _dot(xn, wu_ref[...])
    a = (g * _sigmoid(g)) * u
    acc_ref[...] += _dot(a.astype(BF16), wd_ref[...])

    @pl.when(f == pl.num_programs(1) - 1)
    def _():
        o_ref[...] = x_ref[...] + 0.5 * _rms(acc_ref[...], gpost_ref[...])


def _ffn(x, gpre, wg, wu, wd, gpost, *, tm=512, tf=1408):
    t, d = x.shape
    f = wg.shape[1]
    return pl.pallas_call(
        _ffn_body,
        grid=(t // tm, f // tf),
        in_specs=[
            pl.BlockSpec((tm, d), lambda i, j: (i, 0)),
            pl.BlockSpec((1, d), lambda i, j: (0, 0)),
            pl.BlockSpec((d, tf), lambda i, j: (0, j)),
            pl.BlockSpec((d, tf), lambda i, j: (0, j)),
            pl.BlockSpec((tf, d), lambda i, j: (j, 0)),
            pl.BlockSpec((1, d), lambda i, j: (0, 0)),
        ],
        out_specs=pl.BlockSpec((tm, d), lambda i, j: (i, 0)),
        out_shape=jax.ShapeDtypeStruct((t, d), F32),
        scratch_shapes=[pltpu.VMEM((tm, d), BF16), pltpu.VMEM((tm, d), F32)],
        compiler_params=_params(2),
        name="ffn",
    )(x, gpre, wg, wu, wd, gpost)


def _inproj_body(x_ref, g_ref, w_ref, c_ref, sa_ref, sb_ref,
                 dq_ref, dk_ref, dvt_ref, sq_ref, sk_ref, svt_ref, mq_ref, mk_ref, mvt_ref,
                 gate_ref, kmean_ref):
    xn = _rms(x_ref[...], g_ref[...]).astype(BF16)
    cos, sin_a, sin_b = c_ref[...], sa_ref[...], sb_ref[...]
    scale = HEAD_DIM ** -0.5

    def proj(group):
        return _dot(xn, w_ref[:, group * GROUP_W:(group + 1) * GROUP_W])

    def rope(t):
        half = ROT_DIM // 2
        return (t * cos + pltpu.roll(t, LANES - half, 1) * sin_a
                + pltpu.roll(t, half, 1) * sin_b)

    def emit(out_ref, group, roped, scaled):
        t = proj(group)
        slabs = []
        for s in range(N_GROUPS):
            ts = t[:, s * LANES:(s + 1) * LANES]
            if roped:
                ts = rope(ts)
            slabs.append(ts)
            out_ref[:, s * LANES:(s + 1) * LANES] = (ts * scale if scaled else ts).astype(BF16)
        return slabs

    emit(dq_ref, 0, True, True)
    emit(dk_ref, 1, True, False)
    emit(sq_ref, 3, False, True)
    emit(sk_ref, 4, False, False)
    emit(mq_ref, 6, True, True)
    mk_slabs = emit(mk_ref, 7, True, False)
    for s, ts in enumerate(mk_slabs):
        kmean_ref[0, :, s * LANES:(s + 1) * LANES] = jnp.mean(ts, axis=0, keepdims=True)
    for group, vt_ref in zip(V_GROUPS, (dvt_ref, svt_ref, mvt_ref)):
        vt_ref[0] = lax.dot_general(w_ref[:, group * GROUP_W:(group + 1) * GROUP_W], xn, _TN,
                                    preferred_element_type=F32).astype(BF16)
    for gi in range(N_GATE_GROUPS):
        gate_ref[:, gi * GROUP_W:(gi + 1) * GROUP_W] = _sigmoid(proj(N_QKV_GROUPS + gi)).astype(BF16)


def _inproj(x, g, w, cos_t, sina_t, sinb_t, seq):
    t, d = x.shape
    tm = KEY_TILE
    n_pos_tiles = seq // tm
    qk_spec = pl.BlockSpec((tm, GROUP_W), lambda i: (i, 0))
    vt_spec = pl.BlockSpec((1, GROUP_W, tm), lambda i: (i, 0, 0))
    tab_spec = pl.BlockSpec((tm, LANES), lambda i: (i % n_pos_tiles, 0))
    qk_shape = jax.ShapeDtypeStruct((t, GROUP_W), BF16)
    vt_shape = jax.ShapeDtypeStruct((t // tm, GROUP_W, tm), BF16)
    gate_w = N_GATE_GROUPS * GROUP_W
    return pl.pallas_call(
        _inproj_body,
        grid=(t // tm,),
        in_specs=[
            pl.BlockSpec((tm, d), lambda i: (i, 0)),
            pl.BlockSpec((1, d), lambda i: (0, 0)),
            pl.BlockSpec(w.shape, lambda i: (0, 0), pipeline_mode=pl.Buffered(1)),
            tab_spec, tab_spec, tab_spec,
        ],
        out_specs=[qk_spec, qk_spec, vt_spec] * 3 + [
            pl.BlockSpec((tm, gate_w), lambda i: (i, 0)),
            pl.BlockSpec((1, 1, GROUP_W), lambda i: (i, 0, 0)),
        ],
        out_shape=[qk_shape, qk_shape, vt_shape] * 3 + [
            jax.ShapeDtypeStruct((t, gate_w), BF16),
            jax.ShapeDtypeStruct((t // tm, 1, GROUP_W), F32),
        ],
        compiler_params=_params(1),
        name="inproj",
    )(x, g, w, cos_t, sina_t, sinb_t)


def _stack_heads(q):
    lane = lax.broadcasted_iota(jnp.int32, q.shape, 1)
    zero = jnp.zeros_like(q)
    return jnp.concatenate([jnp.where(lane < HEAD_DIM, q, zero),
                            jnp.where(lane >= HEAD_DIM, q, zero)], axis=0)


def _unstack_heads(o_t, tq):
    return jnp.concatenate([o_t[:HEAD_DIM, :tq], o_t[HEAD_DIM:, tq:]], axis=0).T


def _tile_positions(tk, tq, k0, q0):
    key = lax.broadcasted_iota(jnp.int32, (tk, 2 * tq), 0) + k0
    col = lax.broadcasted_iota(jnp.int32, (tk, 2 * tq), 1)
    query = jnp.where(col >= tq, col - tq, col) + q0
    return key, query


def _pv(vt_ref, j0, p, per_head):
    ones = jnp.ones((SUM_ROWS, KEY_TILE), BF16)
    half = p.shape[1] // 2
    out = None
    for n in range(p.shape[0] // KEY_TILE):
        vt = vt_ref[j0 + n]
        pn = p[n * KEY_TILE:(n + 1) * KEY_TILE]
        if per_head:
            part = jnp.concatenate(
                [_dot(jnp.concatenate([vt[:HEAD_DIM], ones], axis=0), pn[:, :half]),
                 _dot(jnp.concatenate([vt[HEAD_DIM:], ones], axis=0), pn[:, half:])], axis=1)
        else:
            part = _dot(jnp.concatenate([vt, ones], axis=0), pn)
        out = part if out is None else out + part
    return out


def _softmax_step(s, s_max, vt_ref, j0, m_ref, acc_ref, per_head):
    m_prev = m_ref[...]
    m_new = jnp.maximum(m_prev, s_max)
    alpha = jnp.exp(m_prev - m_new)
    acc_ref[...] = alpha * acc_ref[...] + _pv(vt_ref, j0, jnp.exp(s - m_new).astype(BF16), per_head)
    m_ref[...] = m_new


def _softmax_init(m_ref, acc_ref):
    m_ref[...] = jnp.full_like(m_ref, MASKED)
    acc_ref[...] = jnp.zeros_like(acc_ref)


def _softmax_result(acc_ref, n_features):
    return acc_ref[:n_features] / acc_ref[n_features:n_features + 1]


def _col_max(s):
    return jnp.max(s, axis=0, keepdims=True)


def _pipelined_steps(n_regular, streams):
    def ahead(n, buf):
        for logits, _, _, s_ref, cmax_ref in streams:
            s = logits(n)
            s_ref[buf] = s
            cmax_ref[buf] = _col_max(s)

    def consume(n, buf):
        for _, update, _, s_ref, cmax_ref in streams:
            update(n, s_ref[buf], cmax_ref[buf])

    odd = n_regular % 2
    ahead(0, odd)

    @pl.when(odd == 1)
    def _():
        ahead(1, 0)
        consume(0, 1)

    def pair(nn, carry):
        t = odd + 2 * nn
        ahead(t + 1, 1)
        consume(t, 0)
        ahead(t + 2, 0)
        consume(t + 1, 1)
        return carry

    lax.fori_loop(0, n_regular // 2, pair, 0)
    for _, _, final, s_ref, _ in streams:
        final(s_ref[0])


def _step_buffers(tq):
    return [pltpu.VMEM((2, tq, 2 * tq), F32), pltpu.VMEM((2, 1, 2 * tq), F32)]


def _attn_call(body, name, qkv, extra_inputs, extra_specs, scratch, *, batch, seq, tq):
    rows_per_batch = seq // tq
    n_key_tiles = seq // KEY_TILE
    q_spec = pl.BlockSpec((tq, LANES), lambda b, h, i: (b * rows_per_batch + i, h))
    k_spec = pl.BlockSpec((seq, LANES), lambda b, h, i: (b, h))
    vt_spec = pl.BlockSpec((None, n_key_tiles, LANES, KEY_TILE), lambda b, h, i: (b, 0, h, 0))
    operands = []
    for q, k, vt in qkv:
        operands += [q, k, vt.reshape(batch, n_key_tiles, GROUP_W, KEY_TILE)]
    out_shape = jax.ShapeDtypeStruct(qkv[0][0].shape, BF16)
    outs = pl.pallas_call(
        body,
        grid=(batch, N_GROUPS, rows_per_batch),
        in_specs=extra_specs + [q_spec, k_spec, vt_spec] * len(qkv),
        out_specs=[q_spec] * len(qkv),
        out_shape=[out_shape] * len(qkv),
        scratch_shapes=scratch,
        compiler_params=_params(3),
        name=name,
    )(*extra_inputs, *operands)
    return outs if len(qkv) > 1 else outs[0]


def _row_stat(tq):
    return pltpu.VMEM((1, 2 * tq), F32)


def _acc(tq, n_features=LANES):
    return pltpu.VMEM((n_features, 2 * tq), F32)


def _diff_stream(q_ref, k_ref, vt_ref, m_ref, acc_ref, s_ref, cmax_ref, *, tq):
    i = pl.program_id(2)
    qq = _stack_heads(q_ref[...])
    _softmax_init(m_ref, acc_ref)
    tiles = tq // KEY_TILE

    def logits(n):
        k0 = pl.multiple_of(n * tq, tq)
        return _dot_nt(k_ref[pl.ds(k0, tq), :], qq)

    def update(n, s, s_max):
        _softmax_step(s, s_max, vt_ref, n * tiles, m_ref, acc_ref, False)

    def final(s):
        key, query = _tile_positions(tq, tq, 0, 0)
        s = jnp.where(key <= query, s, MASKED)
        _softmax_step(s, _col_max(s), vt_ref, i * tiles, m_ref, acc_ref, False)

    return logits, update, final, s_ref, cmax_ref


def _diff_finish(lam_ref, g_ref, o_ref, acc_ref, *, tq, post_scale):
    o_t = _softmax_result(acc_ref, LANES)
    out = (o_t[:, :tq] - lam_ref[0, 0] * o_t[:, tq:]).T
    o_ref[...] = (_rms(out, g_ref[...]) * post_scale).astype(BF16)


def _mixers_body(lam_ref, g_ref, km_ref, dq_ref, dk_ref, dvt_ref, sq_ref, sk_ref, svt_ref,
                 mq_ref, mk_ref, mvt_ref, a_ref, b_ref, c_ref,
                 dm_ref, dacc_ref, ds_ref, dcmax_ref, mm_ref, macc_ref, ms_ref, mcmax_ref,
                 sbc_ref, sbacc_ref, *, tq, post_scale):
    n_sub = tq // KEY_TILE
    sb = (sq_ref, sk_ref, svt_ref)
    diff = _diff_stream(dq_ref, dk_ref, dvt_ref, dm_ref, dacc_ref, ds_ref, dcmax_ref, tq=tq)
    moba = _moba_stream(km_ref, mq_ref, mk_ref, mvt_ref, mm_ref, macc_ref, ms_ref, mcmax_ref, tq=tq)
    _pipelined_steps(pl.program_id(2), [diff, moba])
    _diff_finish(lam_ref, g_ref, a_ref, dacc_ref, tq=tq, post_scale=post_scale)
    _moba_finish(c_ref, macc_ref, tq=tq)
    c_min = [_sb_start(h, *sb, sbc_ref, sbacc_ref, n_sub=n_sub) for h in range(n_sub)]
    for h in range(n_sub):
        _sb_finish(h, c_min[h], *sb, b_ref, sbc_ref, sbacc_ref, n_sub=n_sub)


def _mixers_attention(diff_qkv, sb_qkv, moba_qkv, lam, gain, kmean, *, batch, seq, post_scale,
                      tq=ATTN_TQ):
    body = functools.partial(_mixers_body, tq=tq, post_scale=post_scale)
    n_sub = tq // KEY_TILE
    return _attn_call(
        body, "mixers_attn", [diff_qkv, sb_qkv, moba_qkv],
        [lam, gain, kmean],
        [pl.BlockSpec(memory_space=pltpu.SMEM), pl.BlockSpec((1, LANES), lambda b, h, i: (0, 0)),
         pl.BlockSpec((1, kmean.shape[1], LANES), lambda b, h, i: (b, 0, h))],
        [_row_stat(tq), _acc(tq, LANES + SUM_ROWS)] + _step_buffers(tq)
        + [_row_stat(tq), _acc(tq, HEAD_DIM + SUM_ROWS)] + _step_buffers(tq)
        + [pltpu.VMEM((n_sub, 1, 2 * KEY_TILE), F32), pltpu.VMEM((n_sub, LANES, 2 * KEY_TILE), F32)],
        batch=batch, seq=seq, tq=tq)


def _sb_later():
    row = lax.broadcasted_iota(jnp.int32, (KEY_TILE + SUM_ROWS, 2 * KEY_TILE), 0)
    col = lax.broadcasted_iota(jnp.int32, row.shape, 1)
    col = jnp.where(col >= KEY_TILE, col - KEY_TILE, col)
    return jnp.logical_or(col > row, row >= KEY_TILE).astype(BF16)


def _sb_sub_tile(h, q_ref, n_sub):
    tq = KEY_TILE
    return pl.program_id(2) * n_sub + h, _stack_heads(q_ref[h * tq:(h + 1) * tq])


def _sb_start(h, q_ref, k_ref, vt_ref, c_ref, acc_ref, *, n_sub):
    i, qq = _sb_sub_tile(h, q_ref, n_sub)
    key, query = _tile_positions(KEY_TILE, KEY_TILE, 0, 0)
    return _sb_first_tiles(i, qq, k_ref, vt_ref, _sb_later(), key < query, c_ref.at[h], acc_ref.at[h])


def _sb_finish(h, c_min, q_ref, k_ref, vt_ref, o_ref, c_ref, acc_ref, *, n_sub):
    tq = KEY_TILE
    i, qq = _sb_sub_tile(h, q_ref, n_sub)
    _sb_earlier_tiles(i, qq, c_min, k_ref, vt_ref, _sb_later(), c_ref.at[h], acc_ref.at[h])
    o_ref[h * tq:(h + 1) * tq] = _unstack_heads(acc_ref[h], tq).astype(BF16)


def _sb_tile_terms(j, qq, k_ref, later, valid):
    tk = KEY_TILE
    k0 = pl.multiple_of(j * tk, tk)
    z = _dot_nt(k_ref[pl.ds(k0, tk), :], qq)
    softplus = jnp.maximum(z, 0.0) + jnp.log(1.0 + jnp.exp2(jnp.abs(z) * -LOG2_E))
    spent = softplus if valid is None else jnp.where(valid, softplus, 0.0)
    hi = spent.astype(BF16)
    lo = (spent - hi.astype(F32)).astype(BF16)
    sums = _dot(later, jnp.concatenate([hi, lo], axis=0))
    return (z - softplus) - sums[:tk], sums[tk:tk + 1]


def _sb_first_tiles(i, qq, k_ref, vt_ref, later, valid, c_ref, acc_ref):
    log_w, spent = _sb_tile_terms(i, qq, k_ref, later, valid)
    w = jnp.where(valid, jnp.exp(log_w), 0.0)
    has_prev = i > 0
    prev = jnp.maximum(i - 1, 0)
    log_w_prev, spent_prev = _sb_tile_terms(prev, qq, k_ref, later, None)
    w_prev = jnp.exp(log_w_prev - spent)
    vt_prev = vt_ref[prev]
    vt_prev = jnp.where(has_prev, vt_prev, jnp.zeros_like(vt_prev))
    acc_ref[...] = _dot(vt_ref[i], w.astype(BF16)) + _dot(vt_prev, w_prev.astype(BF16))
    c = spent + jnp.where(has_prev, spent_prev, 0.0)
    c_ref[...] = c
    return jnp.min(c)


def _sb_earlier_tiles(i, qq, c_min, k_ref, vt_ref, later, c_ref, acc_ref):
    def cond(state):
        j, c_min = state
        return jnp.logical_and(j >= 0, c_min < -SB_LOG_ZERO)

    def body(state):
        j, _ = state
        log_w, spent = _sb_tile_terms(j, qq, k_ref, later, None)
        c = c_ref[...]
        acc_ref[...] += _dot(vt_ref[j], jnp.exp(log_w - c).astype(BF16))
        c_ref[...] = c + spent
        return j - 1, jnp.min(c + spent)

    lax.while_loop(cond, body, (i - 2, c_min))


def _moba_stream(km_ref, q_ref, k_ref, vt_ref, m_ref, acc_ref, s_ref, cmax_ref, *, tq):
    i = pl.program_id(2)
    tk = MOBA_BLOCK
    blocks = tq // tk
    qq = _stack_heads(q_ref[...])

    gate = _dot_nt(km_ref[0], qq)
    blk = lax.broadcasted_iota(jnp.int32, gate.shape, 0)
    col = lax.broadcasted_iota(jnp.int32, (1, 2 * tq), 1)
    own = i * blocks + jnp.where(col >= tq, col - tq, col) // tk
    gate = jnp.where(blk < own, gate, -jnp.inf)
    selected = blk == own
    for _ in range(MOBA_TOPK):
        best = jnp.max(gate, axis=0, keepdims=True)
        idx = jnp.min(jnp.where(gate == best, blk, LANES), axis=0, keepdims=True)
        pick = jnp.logical_and(blk == idx, best > -jnp.inf)
        selected = jnp.logical_or(selected, pick)
        gate = jnp.where(blk == idx, -jnp.inf, gate)
    bias = jnp.where(selected, 0.0, MASKED)
    bias = jnp.concatenate([bias, jnp.zeros((LANES - bias.shape[0], 2 * tq), F32)], axis=0)
    qq = jnp.concatenate([qq, bias.T.astype(BF16)], axis=1)
    key_blk = lax.broadcasted_iota(jnp.int32, (tq, LANES), 0) // tk
    lane = lax.broadcasted_iota(jnp.int32, (tq, LANES), 1)

    _softmax_init(m_ref, acc_ref)

    def logits(n):
        k0 = pl.multiple_of(n * tq, tq)
        one_hot = (lane == key_blk + n * blocks).astype(BF16)
        return _dot_nt(jnp.concatenate([k_ref[pl.ds(k0, tq), :], one_hot], axis=1), qq)

    def update(n, s, s_max):
        _softmax_step(s, s_max, vt_ref, n * blocks, m_ref, acc_ref, True)

    def final(s):
        key, query = _tile_positions(tq, tq, 0, 0)
        s = jnp.where(key <= query, s, MASKED)
        _softmax_step(s, _col_max(s), vt_ref, i * blocks, m_ref, acc_ref, True)

    return logits, update, final, s_ref, cmax_ref


def _moba_finish(o_ref, acc_ref, *, tq):
    o_t = _softmax_result(acc_ref, HEAD_DIM)
    o_ref[...] = jnp.concatenate([o_t[:, :tq], o_t[:, tq:]], axis=0).T.astype(BF16)


def _merge_body(x_ref, a_ref, b_ref, c_ref, gate_ref, wa_ref, wb_ref, wc_ref, wo_ref, g_ref, o_ref):
    d = x_ref.shape[1]
    merged = (gate_ref[:, 0:d] * _dot(a_ref[...], wa_ref[...])
              + gate_ref[:, d:2 * d] * _dot(b_ref[...], wb_ref[...])
              + gate_ref[:, 2 * d:3 * d] * _dot(c_ref[...], wc_ref[...]))
    y = _dot(merged.astype(BF16), wo_ref[...])
    o_ref[...] = x_ref[...] + _rms(y, g_ref[...])


def _merge(x, a, b, c, gates, wa, wb, wc, wo, g, *, tm=512):
    t, d = x.shape
    row = lambda w: pl.BlockSpec((tm, w), lambda i: (i, 0))
    whole = lambda arr: pl.BlockSpec(arr.shape, lambda i: (0, 0))
    return pl.pallas_call(
        _merge_body,
        grid=(t // tm,),
        in_specs=[row(d), row(GROUP_W), row(GROUP_W), row(GROUP_W), row(3 * d),
                  whole(wa), whole(wb), whole(wc), whole(wo), whole(g)],
        out_specs=row(d),
        out_shape=jax.ShapeDtypeStruct((t, d), F32),
        compiler_params=_params(1),
        name="merge",
    )(x, a, b, c, gates, wa, wb, wc, wo, g)


def _rope_lane_tables(seq):
    pos = jnp.arange(seq, dtype=F32)
    inv_freq = ROPE_THETA ** (-jnp.arange(0, ROT_DIM, 2, dtype=F32) / ROT_DIM)
    ang = pos[:, None] * inv_freq[None, :]
    cos, sin = jnp.cos(ang), jnp.sin(ang)
    half = ROT_DIM // 2
    rest = HEAD_DIM - ROT_DIM
    cos_h = jnp.concatenate([cos, cos, jnp.ones((seq, rest), F32)], axis=1)
    sina_h = jnp.concatenate([-sin, jnp.zeros((seq, half + rest), F32)], axis=1)
    sinb_h = jnp.concatenate([jnp.zeros((seq, half), F32), sin, jnp.zeros((seq, rest), F32)], axis=1)
    reps = LANES // HEAD_DIM
    return tuple(jnp.tile(t, (1, reps)) for t in (cos_h, sina_h, sinb_h))


def kernel(x, w_in, w_diff_o, w_sb_o, w_moba_o, w_out, lam_q1, lam_k1, lam_q2, lam_k2,
           diff_norm_g, ffn1_wg, ffn1_wu, ffn1_wd, ffn2_wg, ffn2_wu, ffn2_wd,
           g_ffn1_pre, g_ffn1_post, g_mix_pre, g_mix_post, g_ffn2_pre, g_ffn2_post):
    batch, seq, d = x.shape
    depth = w_in.shape[0]
    n_blocks = seq // MOBA_BLOCK
    assert seq % 512 == 0 and n_blocks <= LANES
    cos_t, sina_t, sinb_t = _rope_lane_tables(seq)
    bf = lambda w: w.astype(BF16)
    vec = lambda g: g.reshape(1, -1)
    xt = x.reshape(batch * seq, d)
    attn = dict(batch=batch, seq=seq)

    for l in range(depth):
        lambda_init = 0.8 - 0.6 * math.exp(-0.3 * l)
        xt = _ffn(xt, vec(g_ffn1_pre[l]), bf(ffn1_wg[l]), bf(ffn1_wu[l]), bf(ffn1_wd[l]),
                  vec(g_ffn1_post[l]))

        (dq, dk, dvt, sq, sk, svt, mq, mk, mvt, gates, kmean) = _inproj(
            xt, vec(g_mix_pre[l]), bf(w_in[l]), cos_t, sina_t, sinb_t, seq)
        lam = (jnp.exp(jnp.sum(lam_q1[l] * lam_k1[l])) - jnp.exp(jnp.sum(lam_q2[l] * lam_k2[l]))
               + lambda_init).reshape(1, 1)
        kmean = kmean.reshape(batch, n_blocks, GROUP_W)
        kmean = jnp.pad(kmean, ((0, 0), (0, -n_blocks % BF16_ROWS), (0, 0))).astype(BF16)
        a_out, b_out, c_out = _mixers_attention(
            (dq, dk, dvt), (sq, sk, svt), (mq, mk, mvt), lam, vec(diff_norm_g[l]), kmean,
            post_scale=1.0 - lambda_init, **attn)

        xt = _merge(xt, a_out, b_out, c_out, gates, bf(w_diff_o[l]), bf(w_sb_o[l]), bf(w_moba_o[l]),
                    bf(w_out[l]), vec(g_mix_post[l]))

        xt = _ffn(xt, vec(g_ffn2_pre[l]), bf(ffn2_wg[l]), bf(ffn2_wu[l]), bf(ffn2_wd[l]),
                  vec(g_ffn2_post[l]))
    return xt.reshape(batch, seq, d)
```

```python
import functools
import math

import jax
import jax.numpy as jnp
from jax import lax
from jax.experimental import pallas as pl
from jax.experimental.pallas import tpu as pltpu

F32 = jnp.float32
BF16 = jnp.bfloat16

LANES = 128
HEAD_DIM = 64
ROT_DIM = HEAD_DIM // 4
ROPE_THETA = 500000.0
MOBA_BLOCK = 256
MOBA_TOPK = 3
NORM_EPS = 1e-6
GROUP_W = 512
N_GROUPS = GROUP_W // LANES
N_QKV_GROUPS = 9
N_GATE_GROUPS = 6
V_GROUPS = (2, 5, 8)
KEY_TILE = MOBA_BLOCK
SB_LOG_ZERO = -104.0
LOG2_E = math.log2(math.e)
MASKED = -1e30
BF16_ROWS = 16
SUM_ROWS = BF16_ROWS
ATTN_TQ = 512
VMEM_LIMIT = 56 * 1024 * 1024

_NT = (((1,), (1,)), ((), ()))
_TN = (((0,), (1,)), ((), ()))


def _dot(a, b):
    return jnp.dot(a, b, preferred_element_type=F32)


def _dot_nt(a, b):
    return lax.dot_general(a, b, _NT, preferred_element_type=F32)


def _sigmoid(x):
    return 1.0 / (1.0 + jnp.exp(-x))


def _rms(x, gain):
    return x * lax.rsqrt(jnp.mean(x * x, axis=-1, keepdims=True) + NORM_EPS) * gain


def _params(n_axes):
    return pltpu.CompilerParams(dimension_semantics=("arbitrary",) * n_axes,
                                vmem_limit_bytes=VMEM_LIMIT)


def _ffn_body(x_ref, gpre_ref, wg_ref, wu_ref, wd_ref, gpost_ref, o_ref, *, tf):
    x = x_ref[...]
    xn = _rms(x, gpre_ref[...]).astype(BF16)
    y = None
    for c in range(wg_ref.shape[1] // tf):
        cols = slice(c * tf, (c + 1) * tf)
        g = _dot(xn, wg_ref[:, cols])
        u = _dot(xn, wu_ref[:, cols])
        part = _dot(((g * _sigmoid(g)) * u).astype(BF16), wd_ref[cols, :])
        y = part if y is None else y + part
    o_ref[...] = x + 0.5 * _rms(y, gpost_ref[...])


def _ffn(x, gpre, wg, wu, wd, gpost, *, tm=512, tf=1408):
    t, d = x.shape
    whole = lambda arr: pl.BlockSpec(arr.shape, lambda i: (0, 0), pipeline_mode=pl.Buffered(1))
    return pl.pallas_call(
        functools.partial(_ffn_body, tf=tf),
        grid=(t // tm,),
        in_specs=[
            pl.BlockSpec((tm, d), lambda i: (i, 0)),
            pl.BlockSpec((1, d), lambda i: (0, 0)),
            whole(wg), whole(wu), whole(wd),
            pl.BlockSpec((1, d), lambda i: (0, 0)),
        ],
        out_specs=pl.BlockSpec((tm, d), lambda i: (i, 0)),
        out_shape=jax.ShapeDtypeStruct((t, d), F32),
        compiler_params=_params(1),
        name="ffn",
    )(x, gpre, wg, wu, wd, gpost)


def _inproj_body(x_ref, g_ref, w_ref, c_ref, sa_ref, sb_ref,
                 dq_ref, dk_ref, dvt_ref, sq_ref, sk_ref, svt_ref, mq_ref, mk_ref, mvt_ref,
                 gate_ref, kmean_ref):
    xn = _rms(x_ref[...], g_ref[...]).astype(BF16)
    cos, sin_a, sin_b = c_ref[...], sa_ref[...], sb_ref[...]
    scale = HEAD_DIM ** -0.5

    def proj(group):
        return _dot(xn, w_ref[:, group * GROUP_W:(group + 1) * GROUP_W])

    def rope(t):
        half = ROT_DIM // 2
        return (t * cos + pltpu.roll(t, LANES - half, 1) * sin_a
                + pltpu.roll(t, half, 1) * sin_b)

    def emit(out_ref, group, roped, scaled):
        t = proj(group)
        slabs = []
        for s in range(N_GROUPS):
            ts = t[:, s * LANES:(s + 1) * LANES]
            if roped:
                ts = rope(ts)
            slabs.append(ts)
            out_ref[:, s * LANES:(s + 1) * LANES] = (ts * scale if scaled else ts).astype(BF16)
        return slabs

    emit(dq_ref, 0, True, True)
    emit(dk_ref, 1, True, False)
    emit(sq_ref, 3, False, True)
    emit(sk_ref, 4, False, False)
    emit(mq_ref, 6, True, True)
    mk_slabs = emit(mk_ref, 7, True, False)
    for s, ts in enumerate(mk_slabs):
        kmean_ref[0, :, s * LANES:(s + 1) * LANES] = jnp.mean(ts, axis=0, keepdims=True)
    for group, vt_ref in zip(V_GROUPS, (dvt_ref, svt_ref, mvt_ref)):
        vt_ref[0] = lax.dot_general(w_ref[:, group * GROUP_W:(group + 1) * GROUP_W], xn, _TN,
                                    preferred_element_type=F32).astype(BF16)
    for gi in range(N_GATE_GROUPS):
        gate_ref[:, gi * GROUP_W:(gi + 1) * GROUP_W] = _sigmoid(proj(N_QKV_GROUPS + gi)).astype(BF16)


def _inproj(x, g, w, cos_t, sina_t, sinb_t, seq):
    t, d = x.shape
    tm = KEY_TILE
    n_pos_tiles = seq // tm
    qk_spec = pl.BlockSpec((tm, GROUP_W), lambda i: (i, 0))
    vt_spec = pl.BlockSpec((1, GROUP_W, tm), lambda i: (i, 0, 0))
    tab_spec = pl.BlockSpec((tm, LANES), lambda i: (i % n_pos_tiles, 0))
    qk_shape = jax.ShapeDtypeStruct((t, GROUP_W), BF16)
    vt_shape = jax.ShapeDtypeStruct((t // tm, GROUP_W, tm), BF16)
    gate_w = N_GATE_GROUPS * GROUP_W
    return pl.pallas_call(
        _inproj_body,
        grid=(t // tm,),
        in_specs=[
            pl.BlockSpec((tm, d), lambda i: (i, 0)),
            pl.BlockSpec((1, d), lambda i: (0, 0)),
            pl.BlockSpec(w.shape, lambda i: (0, 0), pipeline_mode=pl.Buffered(1)),
            tab_spec, tab_spec, tab_spec,
        ],
        out_specs=[qk_spec, qk_spec, vt_spec] * 3 + [
            pl.BlockSpec((tm, gate_w), lambda i: (i, 0)),
            pl.BlockSpec((1, 1, GROUP_W), lambda i: (i, 0, 0)),
        ],
        out_shape=[qk_shape, qk_shape, vt_shape] * 3 + [
            jax.ShapeDtypeStruct((t, gate_w), BF16),
            jax.ShapeDtypeStruct((t // tm, 1, GROUP_W), F32),
        ],
        compiler_params=_params(1),
        name="inproj",
    )(x, g, w, cos_t, sina_t, sinb_t)


def _stack_heads(q):
    lane = lax.broadcasted_iota(jnp.int32, q.shape, 1)
    zero = jnp.zeros_like(q)
    return jnp.concatenate([jnp.where(lane < HEAD_DIM, q, zero),
                            jnp.where(lane >= HEAD_DIM, q, zero)], axis=0)


def _unstack_heads(o_t, tq):
    return jnp.concatenate([o_t[:HEAD_DIM, :tq], o_t[HEAD_DIM:, tq:]], axis=0).T


def _tile_positions(tk, tq, k0, q0):
    key = lax.broadcasted_iota(jnp.int32, (tk, 2 * tq), 0) + k0
    col = lax.broadcasted_iota(jnp.int32, (tk, 2 * tq), 1)
    query = jnp.where(col >= tq, col - tq, col) + q0
    return key, query


def _pv(vt_ref, j0, p, per_head):
    ones = jnp.ones((SUM_ROWS, KEY_TILE), BF16)
    half = p.shape[1] // 2
    out = None
    for n in range(p.shape[0] // KEY_TILE):
        vt = vt_ref[j0 + n]
        pn = p[n * KEY_TILE:(n + 1) * KEY_TILE]
        if per_head:
            part = jnp.concatenate(
                [_dot(jnp.concatenate([vt[:HEAD_DIM], ones], axis=0), pn[:, :half]),
                 _dot(jnp.concatenate([vt[HEAD_DIM:], ones], axis=0), pn[:, half:])], axis=1)
        else:
            part = _dot(jnp.concatenate([vt, ones], axis=0), pn)
        out = part if out is None else out + part
    return out


def _softmax_step(s, s_max, vt_ref, j0, m_ref, acc_ref, per_head):
    m_prev = m_ref[...]
    m_new = jnp.maximum(m_prev, s_max)
    alpha = jnp.exp(m_prev - m_new)
    acc_ref[...] = alpha * acc_ref[...] + _pv(vt_ref, j0, jnp.exp(s - m_new).astype(BF16), per_head)
    m_ref[...] = m_new


def _softmax_init(m_ref, acc_ref):
    m_ref[...] = jnp.full_like(m_ref, MASKED)
    acc_ref[...] = jnp.zeros_like(acc_ref)


def _softmax_result(acc_ref, n_features):
    return acc_ref[:n_features] / acc_ref[n_features:n_features + 1]


def _col_max(s):
    return jnp.max(s, axis=0, keepdims=True)


def _pipelined_steps(n_regular, streams):
    def ahead(n, buf):
        for logits, _, _, s_ref, cmax_ref in streams:
            s = logits(n)
            s_ref[buf] = s
            cmax_ref[buf] = _col_max(s)

    def consume(n, buf):
        for _, update, _, s_ref, cmax_ref in streams:
            update(n, s_ref[buf], cmax_ref[buf])

    odd = n_regular % 2
    ahead(0, odd)

    @pl.when(odd == 1)
    def _():
        ahead(1, 0)
        consume(0, 1)

    def pair(nn, carry):
        t = odd + 2 * nn
        ahead(t + 1, 1)
        consume(t, 0)
        ahead(t + 2, 0)
        consume(t + 1, 1)
        return carry

    lax.fori_loop(0, n_regular // 2, pair, 0)
    for _, _, final, s_ref, _ in streams:
        final(s_ref[0])


def _step_buffers(tq):
    return [pltpu.VMEM((2, tq, 2 * tq), F32), pltpu.VMEM((2, 1, 2 * tq), F32)]


def _attn_call(body, name, qkv, extra_inputs, extra_specs, scratch, *, batch, seq, tq):
    rows_per_batch = seq // tq
    n_key_tiles = seq // KEY_TILE
    q_spec = pl.BlockSpec((tq, LANES), lambda b, h, i: (b * rows_per_batch + i, h))
    k_spec = pl.BlockSpec((seq, LANES), lambda b, h, i: (b, h))
    vt_spec = pl.BlockSpec((None, n_key_tiles, LANES, KEY_TILE), lambda b, h, i: (b, 0, h, 0))
    operands = []
    for q, k, vt in qkv:
        operands += [q, k, vt.reshape(batch, n_key_tiles, GROUP_W, KEY_TILE)]
    out_shape = jax.ShapeDtypeStruct(qkv[0][0].shape, BF16)
    outs = pl.pallas_call(
        body,
        grid=(batch, N_GROUPS, rows_per_batch),
        in_specs=extra_specs + [q_spec, k_spec, vt_spec] * len(qkv),
        out_specs=[q_spec] * len(qkv),
        out_shape=[out_shape] * len(qkv),
        scratch_shapes=scratch,
        compiler_params=_params(3),
        name=name,
    )(*extra_inputs, *operands)
    return outs if len(qkv) > 1 else outs[0]


def _row_stat(tq):
    return pltpu.VMEM((1, 2 * tq), F32)


def _acc(tq, n_features=LANES):
    return pltpu.VMEM((n_features, 2 * tq), F32)


def _diff_stream(q_ref, k_ref, vt_ref, m_ref, acc_ref, s_ref, cmax_ref, *, tq):
    i = pl.program_id(2)
    qq = _stack_heads(q_ref[...])
    _softmax_init(m_ref, acc_ref)
    tiles = tq // KEY_TILE

    def logits(n):
        k0 = pl.multiple_of(n * tq, tq)
        return _dot_nt(k_ref[pl.ds(k0, tq), :], qq)

    def update(n, s, s_max):
        _softmax_step(s, s_max, vt_ref, n * tiles, m_ref, acc_ref, False)

    def final(s):
        key, query = _tile_positions(tq, tq, 0, 0)
        s = jnp.where(key <= query, s, MASKED)
        _softmax_step(s, _col_max(s), vt_ref, i * tiles, m_ref, acc_ref, False)

    return logits, update, final, s_ref, cmax_ref


def _diff_finish(lam_ref, g_ref, o_ref, acc_ref, *, tq, post_scale):
    o_t = _softmax_result(acc_ref, LANES)
    out = (o_t[:, :tq] - lam_ref[0, 0] * o_t[:, tq:]).T
    o_ref[...] = (_rms(out, g_ref[...]) * post_scale).astype(BF16)


def _mixers_body(lam_ref, g_ref, km_ref, dq_ref, dk_ref, dvt_ref, sq_ref, sk_ref, svt_ref,
                 mq_ref, mk_ref, mvt_ref, a_ref, b_ref, c_ref,
                 dm_ref, dacc_ref, ds_ref, dcmax_ref, mm_ref, macc_ref, ms_ref, mcmax_ref,
                 sbc_ref, sbacc_ref, *, tq, post_scale):
    n_sub = tq // KEY_TILE
    sb = (sq_ref, sk_ref, svt_ref)
    diff = _diff_stream(dq_ref, dk_ref, dvt_ref, dm_ref, dacc_ref, ds_ref, dcmax_ref, tq=tq)
    moba = _moba_stream(km_ref, mq_ref, mk_ref, mvt_ref, mm_ref, macc_ref, ms_ref, mcmax_ref, tq=tq)
    _pipelined_steps(pl.program_id(2), [diff, moba])
    _diff_finish(lam_ref, g_ref, a_ref, dacc_ref, tq=tq, post_scale=post_scale)
    _moba_finish(c_ref, macc_ref, tq=tq)
    c_min = [_sb_start(h, *sb, sbc_ref, sbacc_ref, n_sub=n_sub) for h in range(n_sub)]
    for h in range(n_sub):
        _sb_finish(h, c_min[h], *sb, b_ref, sbc_ref, sbacc_ref, n_sub=n_sub)


def _mixers_attention(diff_qkv, sb_qkv, moba_qkv, lam, gain, kmean, *, batch, seq, post_scale,
                      tq=ATTN_TQ):
    body = functools.partial(_mixers_body, tq=tq, post_scale=post_scale)
    n_sub = tq // KEY_TILE
    return _attn_call(
        body, "mixers_attn", [diff_qkv, sb_qkv, moba_qkv],
        [lam, gain, kmean],
        [pl.BlockSpec(memory_space=pltpu.SMEM), pl.BlockSpec((1, LANES), lambda b, h, i: (0, 0)),
         pl.BlockSpec((1, kmean.shape[1], LANES), lambda b, h, i: (b, 0, h))],
        [_row_stat(tq), _acc(tq, LANES + SUM_ROWS)] + _step_buffers(tq)
        + [_row_stat(tq), _acc(tq, HEAD_DIM + SUM_ROWS)] + _step_buffers(tq)
        + [pltpu.VMEM((n_sub, 1, 2 * KEY_TILE), F32), pltpu.VMEM((n_sub, LANES, 2 * KEY_TILE), F32)],
        batch=batch, seq=seq, tq=tq)


def _sb_later():
    row = lax.broadcasted_iota(jnp.int32, (KEY_TILE + SUM_ROWS, 2 * KEY_TILE), 0)
    col = lax.broadcasted_iota(jnp.int32, row.shape, 1)
    col = jnp.where(col >= KEY_TILE, col - KEY_TILE, col)
    return jnp.logical_or(col > row, row >= KEY_TILE).astype(BF16)


def _sb_sub_tile(h, q_ref, n_sub):
    tq = KEY_TILE
    return pl.program_id(2) * n_sub + h, _stack_heads(q_ref[h * tq:(h + 1) * tq])


def _sb_start(h, q_ref, k_ref, vt_ref, c_ref, acc_ref, *, n_sub):
    i, qq = _sb_sub_tile(h, q_ref, n_sub)
    key, query = _tile_positions(KEY_TILE, KEY_TILE, 0, 0)
    return _sb_first_tiles(i, qq, k_ref, vt_ref, _sb_later(), key < query, c_ref.at[h], acc_ref.at[h])


def _sb_finish(h, c_min, q_ref, k_ref, vt_ref, o_ref, c_ref, acc_ref, *, n_sub):
    tq = KEY_TILE
    i, qq = _sb_sub_tile(h, q_ref, n_sub)
    _sb_earlier_tiles(i, qq, c_min, k_ref, vt_ref, _sb_later(), c_ref.at[h], acc_ref.at[h])
    o_ref[h * tq:(h + 1) * tq] = _unstack_heads(acc_ref[h], tq).astype(BF16)


def _sb_tile_terms(j, qq, k_ref, later, valid):
    tk = KEY_TILE
    k0 = pl.multiple_of(j * tk, tk)
    z = _dot_nt(k_ref[pl.ds(k0, tk), :], qq)
    softplus = jnp.maximum(z, 0.0) + jnp.log(1.0 + jnp.exp2(jnp.abs(z) * -LOG2_E))
    spent = softplus if valid is None else jnp.where(valid, softplus, 0.0)
    hi = spent.astype(BF16)
    lo = (spent - hi.astype(F32)).astype(BF16)
    sums = _dot(later, jnp.concatenate([hi, lo], axis=0))
    return (z - softplus) - sums[:tk], sums[tk:tk + 1]


def _sb_first_tiles(i, qq, k_ref, vt_ref, later, valid, c_ref, acc_ref):
    log_w, spent = _sb_tile_terms(i, qq, k_ref, later, valid)
    w = jnp.where(valid, jnp.exp(log_w), 0.0)
    has_prev = i > 0
    prev = jnp.maximum(i - 1, 0)
    log_w_prev, spent_prev = _sb_tile_terms(prev, qq, k_ref, later, None)
    w_prev = jnp.exp(log_w_prev - spent)
    vt_prev = vt_ref[prev]
    vt_prev = jnp.where(has_prev, vt_prev, jnp.zeros_like(vt_prev))
    acc_ref[...] = _dot(vt_ref[i], w.astype(BF16)) + _dot(vt_prev, w_prev.astype(BF16))
    c = spent + jnp.where(has_prev, spent_prev, 0.0)
    c_ref[...] = c
    return jnp.min(c)


def _sb_earlier_tiles(i, qq, c_min, k_ref, vt_ref, later, c_ref, acc_ref):
    def cond(state):
        j, c_min = state
        return jnp.logical_and(j >= 0, c_min < -SB_LOG_ZERO)

    def body(state):
        j, _ = state
        log_w, spent = _sb_tile_terms(j, qq, k_ref, later, None)
        c = c_ref[...]
        acc_ref[...] += _dot(vt_ref[j], jnp.exp(log_w - c).astype(BF16))
        c_ref[...] = c + spent
        return j - 1, jnp.min(c + spent)

    lax.while_loop(cond, body, (i - 2, c_min))


def _moba_stream(km_ref, q_ref, k_ref, vt_ref, m_ref, acc_ref, s_ref, cmax_ref, *, tq):
    i = pl.program_id(2)
    tk = MOBA_BLOCK
    blocks = tq // tk
    qq = _stack_heads(q_ref[...])

    gate = _dot_nt(km_ref[0], qq)
    blk = lax.broadcasted_iota(jnp.int32, gate.shape, 0)
    col = lax.broadcasted_iota(jnp.int32, (1, 2 * tq), 1)
    own = i * blocks + jnp.where(col >= tq, col - tq, col) // tk
    gate = jnp.where(blk < own, gate, -jnp.inf)
    selected = blk == own
    for _ in range(MOBA_TOPK):
        best = jnp.max(gate, axis=0, keepdims=True)
        idx = jnp.min(jnp.where(gate == best, blk, LANES), axis=0, keepdims=True)
        pick = jnp.logical_and(blk == idx, best > -jnp.inf)
        selected = jnp.logical_or(selected, pick)
        gate = jnp.where(blk == idx, -jnp.inf, gate)
    bias = jnp.where(selected, 0.0, MASKED)
    bias = jnp.concatenate([bias, jnp.zeros((LANES - bias.shape[0], 2 * tq), F32)], axis=0)
    qq = jnp.concatenate([qq, bias.T.astype(BF16)], axis=1)
    key_blk = lax.broadcasted_iota(jnp.int32, (tq, LANES), 0) // tk
    lane = lax.broadcasted_iota(jnp.int32, (tq, LANES), 1)

    _softmax_init(m_ref, acc_ref)

    def logits(n):
        k0 = pl.multiple_of(n * tq, tq)
        one_hot = (lane == key_blk + n * blocks).astype(BF16)
        return _dot_nt(jnp.concatenate([k_ref[pl.ds(k0, tq), :], one_hot], axis=1), qq)

    def update(n, s, s_max):
        _softmax_step(s, s_max, vt_ref, n * blocks, m_ref, acc_ref, True)

    def final(s):
        key, query = _tile_positions(tq, tq, 0, 0)
        s = jnp.where(key <= query, s, MASKED)
        _softmax_step(s, _col_max(s), vt_ref, i * blocks, m_ref, acc_ref, True)

    return logits, update, final, s_ref, cmax_ref


def _moba_finish(o_ref, acc_ref, *, tq):
    o_t = _softmax_result(acc_ref, HEAD_DIM)
    o_ref[...] = jnp.concatenate([o_t[:, :tq], o_t[:, tq:]], axis=0).T.astype(BF16)


def _merge_body(x_ref, a_ref, b_ref, c_ref, gate_ref, wa_ref, wb_ref, wc_ref, wo_ref, g_ref, o_ref):
    d = x_ref.shape[1]
    merged = (gate_ref[:, 0:d] * _dot(a_ref[...], wa_ref[...])
              + gate_ref[:, d:2 * d] * _dot(b_ref[...], wb_ref[...])
              + gate_ref[:, 2 * d:3 * d] * _dot(c_ref[...], wc_ref[...]))
    y = _dot(merged.astype(BF16), wo_ref[...])
    o_ref[...] = x_ref[...] + _rms(y, g_ref[...])


def _merge(x, a, b, c, gates, wa, wb, wc, wo, g, *, tm=512):
    t, d = x.shape
    row = lambda w: pl.BlockSpec((tm, w), lambda i: (i, 0))
    whole = lambda arr: pl.BlockSpec(arr.shape, lambda i: (0, 0))
    return pl.pallas_call(
        _merge_body,
        grid=(t // tm,),
        in_specs=[row(d), row(GROUP_W), row(GROUP_W), row(GROUP_W), row(3 * d),
                  whole(wa), whole(wb), whole(wc), whole(wo), whole(g)],
        out_specs=row(d),
        out_shape=jax.ShapeDtypeStruct((t, d), F32),
        compiler_params=_params(1),
        name="merge",
    )(x, a, b, c, gates, wa, wb, wc, wo, g)


def _rope_lane_tables(seq):
    pos = jnp.arange(seq, dtype=F32)
    inv_freq = ROPE_THETA ** (-jnp.arange(0, ROT_DIM, 2, dtype=F32) / ROT_DIM)
    ang = pos[:, None] * inv_freq[None, :]
    cos, sin = jnp.cos(ang), jnp.sin(ang)
    half = ROT_DIM // 2
    rest = HEAD_DIM - ROT_DIM
    cos_h = jnp.concatenate([cos, cos, jnp.ones((seq, rest), F32)], axis=1)
    sina_h = jnp.concatenate([-sin, jnp.zeros((seq, half + rest), F32)], axis=1)
    sinb_h = jnp.concatenate([jnp.zeros((seq, half), F32), sin, jnp.zeros((seq, rest), F32)], axis=1)
    reps = LANES // HEAD_DIM
    return tuple(jnp.tile(t, (1, reps)) for t in (cos_h, sina_h, sinb_h))


def kernel(x, w_in, w_diff_o, w_sb_o, w_moba_o, w_out, lam_q1, lam_k1, lam_q2, lam_k2,
           diff_norm_g, ffn1_wg, ffn1_wu, ffn1_wd, ffn2_wg, ffn2_wu, ffn2_wd,
           g_ffn1_pre, g_ffn1_post, g_mix_pre, g_mix_post, g_ffn2_pre, g_ffn2_post):
    batch, seq, d = x.shape
    depth = w_in.shape[0]
    n_blocks = seq // MOBA_BLOCK
    assert seq % 512 == 0 and n_blocks <= LANES
    cos_t, sina_t, sinb_t = _rope_lane_tables(seq)
    bf = lambda w: w.astype(BF16)
    vec = lambda g: g.reshape(1, -1)
    xt = x.reshape(batch * seq, d)
    attn = dict(batch=batch, seq=seq)

    for l in range(depth):
        lambda_init = 0.8 - 0.6 * math.exp(-0.3 * l)
        xt = _ffn(xt, vec(g_ffn1_pre[l]), bf(ffn1_wg[l]), bf(ffn1_wu[l]), bf(ffn1_wd[l]),
                  vec(g_ffn1_post[l]))

        (dq, dk, dvt, sq, sk, svt, mq, mk, mvt, gates, kmean) = _inproj(
            xt, vec(g_mix_pre[l]), bf(w_in[l]), cos_t, sina_t, sinb_t, seq)
        lam = (jnp.exp(jnp.sum(lam_q1[l] * lam_k1[l])) - jnp.exp(jnp.sum(lam_q2[l] * lam_k2[l]))
               + lambda_init).reshape(1, 1)
        kmean = kmean.reshape(batch, n_blocks, GROUP_W)
        kmean = jnp.pad(kmean, ((0, 0), (0, -n_blocks % BF16_ROWS), (0, 0))).astype(BF16)
        a_out, b_out, c_out = _mixers_attention(
            (dq, dk, dvt), (sq, sk, svt), (mq, mk, mvt), lam, vec(diff_norm_g[l]), kmean,
            post_scale=1.0 - lambda_init, **attn)

        xt = _merge(xt, a_out, b_out, c_out, gates, bf(w_diff_o[l]), bf(w_sb_o[l]), bf(w_moba_o[l]),
                    bf(w_out[l]), vec(g_mix_post[l]))

        xt = _ffn(xt, vec(g_ffn2_pre[l]), bf(ffn2_wg[l]), bf(ffn2_wu[l]), bf(ffn2_wd[l]),
                  vec(g_ffn2_post[l]))
    return xt.reshape(batch, seq, d)
```

```python
import functools
import math

import jax
import jax.numpy as jnp
from jax import lax
from jax.experimental import pallas as pl
from jax.experimental.pallas import tpu as pltpu

F32 = jnp.float32
BF16 = jnp.bfloat16

LANES = 128
HEAD_DIM = 64
ROT_DIM = HEAD_DIM // 4
ROPE_THETA = 500000.0
MOBA_BLOCK = 256
MOBA_TOPK = 3
NORM_EPS = 1e-6
GROUP_W = 512
N_GROUPS = GROUP_W // LANES
N_QKV_GROUPS = 9
N_GATE_GROUPS = 6
V_GROUPS = (2, 5, 8)
KEY_TILE = MOBA_BLOCK
SB_LOG_ZERO = -104.0
LOG2_E = math.log2(math.e)
MASKED = -1e30
BF16_ROWS = 16
SUM_ROWS = BF16_ROWS
ATTN_TQ = 512
VMEM_LIMIT = 56 * 1024 * 1024

_NT = (((1,), (1,)), ((), ()))
_TN = (((0,), (1,)), ((), ()))


def _dot(a, b):
    return jnp.dot(a, b, preferred_element_type=F32)


def _dot_nt(a, b):
    return lax.dot_general(a, b, _NT, preferred_element_type=F32)


def _sigmoid(x):
    return 1.0 / (1.0 + jnp.exp(-x))


def _rms(x, gain):
    return x * lax.rsqrt(jnp.mean(x * x, axis=-1, keepdims=True) + NORM_EPS) * gain


def _params(n_axes):
    return pltpu.CompilerParams(dimension_semantics=("arbitrary",) * n_axes,
                                vmem_limit_bytes=VMEM_LIMIT)


def _ffn_body(x_ref, gpre_ref, wg_ref, wu_ref, wd_ref, gpost_ref, o_ref, *, tf):
    x = x_ref[...]
    xn = _rms(x, gpre_ref[...]).astype(BF16)
    y = None
    for c in range(wg_ref.shape[1] // tf):
        cols = slice(c * tf, (c + 1) * tf)
        g = _dot(xn, wg_ref[:, cols])
        u = _dot(xn, wu_ref[:, cols])
        part = _dot(((g * _sigmoid(g)) * u).astype(BF16), wd_ref[cols, :])
        y = part if y is None else y + part
    o_ref[...] = x + 0.5 * _rms(y, gpost_ref[...])


def _layer_spec(stacked, layer):
    return pl.BlockSpec((None,) + stacked.shape[1:], lambda i: (layer, 0, 0),
                        pipeline_mode=pl.Buffered(1))


def _ffn(x, gpre, wg, wu, wd, gpost, layer, *, tm=512, tf=1408):
    t, d = x.shape
    return pl.pallas_call(
        functools.partial(_ffn_body, tf=tf),
        grid=(t // tm,),
        in_specs=[
            pl.BlockSpec((tm, d), lambda i: (i, 0)),
            pl.BlockSpec((1, d), lambda i: (0, 0)),
            _layer_spec(wg, layer), _layer_spec(wu, layer), _layer_spec(wd, layer),
            pl.BlockSpec((1, d), lambda i: (0, 0)),
        ],
        out_specs=pl.BlockSpec((tm, d), lambda i: (i, 0)),
        out_shape=jax.ShapeDtypeStruct((t, d), F32),
        compiler_params=_params(1),
        name="ffn",
    )(x, gpre, wg, wu, wd, gpost)


def _inproj_body(x_ref, g_ref, w_ref, c_ref, sa_ref, sb_ref,
                 dq_ref, dk_ref, dvt_ref, sq_ref, sk_ref, svt_ref, mq_ref, mk_ref, mvt_ref,
                 gate_ref, kmean_ref):
    xn = _rms(x_ref[...], g_ref[...]).astype(BF16)
    cos, sin_a, sin_b = c_ref[...], sa_ref[...], sb_ref[...]
    scale = HEAD_DIM ** -0.5

    def proj(group):
        return _dot(xn, w_ref[:, group * GROUP_W:(group + 1) * GROUP_W])

    def rope(t):
        half = ROT_DIM // 2
        return (t * cos + pltpu.roll(t, LANES - half, 1) * sin_a
                + pltpu.roll(t, half, 1) * sin_b)

    def emit(out_ref, group, roped, scaled):
        t = proj(group)
        slabs = []
        for s in range(N_GROUPS):
            ts = t[:, s * LANES:(s + 1) * LANES]
            if roped:
                ts = rope(ts)
            slabs.append(ts)
            out_ref[:, s * LANES:(s + 1) * LANES] = (ts * scale if scaled else ts).astype(BF16)
        return slabs

    emit(dq_ref, 0, True, True)
    emit(dk_ref, 1, True, False)
    emit(sq_ref, 3, False, True)
    emit(sk_ref, 4, False, False)
    emit(mq_ref, 6, True, True)
    mk_slabs = emit(mk_ref, 7, True, False)
    for s, ts in enumerate(mk_slabs):
        kmean_ref[0, :, s * LANES:(s + 1) * LANES] = jnp.mean(ts, axis=0, keepdims=True)
    for group, vt_ref in zip(V_GROUPS, (dvt_ref, svt_ref, mvt_ref)):
        vt_ref[0] = lax.dot_general(w_ref[:, group * GROUP_W:(group + 1) * GROUP_W], xn, _TN,
                                    preferred_element_type=F32).astype(BF16)
    for gi in range(N_GATE_GROUPS):
        gate_ref[:, gi * GROUP_W:(gi + 1) * GROUP_W] = _sigmoid(proj(N_QKV_GROUPS + gi)).astype(BF16)


def _inproj(x, g, w, layer, cos_t, sina_t, sinb_t, seq):
    t, d = x.shape
    tm = KEY_TILE
    n_pos_tiles = seq // tm
    qk_spec = pl.BlockSpec((tm, GROUP_W), lambda i: (i, 0))
    vt_spec = pl.BlockSpec((1, GROUP_W, tm), lambda i: (i, 0, 0))
    tab_spec = pl.BlockSpec((tm, LANES), lambda i: (i % n_pos_tiles, 0))
    qk_shape = jax.ShapeDtypeStruct((t, GROUP_W), BF16)
    vt_shape = jax.ShapeDtypeStruct((t // tm, GROUP_W, tm), BF16)
    gate_w = N_GATE_GROUPS * GROUP_W
    return pl.pallas_call(
        _inproj_body,
        grid=(t // tm,),
        in_specs=[
            pl.BlockSpec((tm, d), lambda i: (i, 0)),
            pl.BlockSpec((1, d), lambda i: (0, 0)),
            _layer_spec(w, layer),
            tab_spec, tab_spec, tab_spec,
        ],
        out_specs=[qk_spec, qk_spec, vt_spec] * 3 + [
            pl.BlockSpec((tm, gate_w), lambda i: (i, 0)),
            pl.BlockSpec((1, 1, GROUP_W), lambda i: (i, 0, 0)),
        ],
        out_shape=[qk_shape, qk_shape, vt_shape] * 3 + [
            jax.ShapeDtypeStruct((t, gate_w), BF16),
            jax.ShapeDtypeStruct((t // tm, 1, GROUP_W), F32),
        ],
        compiler_params=_params(1),
        name="inproj",
    )(x, g, w, cos_t, sina_t, sinb_t)


def _stack_heads(q):
    lane = lax.broadcasted_iota(jnp.int32, q.shape, 1)
    zero = jnp.zeros_like(q)
    return jnp.concatenate([jnp.where(lane < HEAD_DIM, q, zero),
                            jnp.where(lane >= HEAD_DIM, q, zero)], axis=0)


def _unstack_heads(o_t, tq):
    return jnp.concatenate([o_t[:HEAD_DIM, :tq], o_t[HEAD_DIM:, tq:]], axis=0).T


def _tile_positions(tk, tq, k0, q0):
    key = lax.broadcasted_iota(jnp.int32, (tk, 2 * tq), 0) + k0
    col = lax.broadcasted_iota(jnp.int32, (tk, 2 * tq), 1)
    query = jnp.where(col >= tq, col - tq, col) + q0
    return key, query


def _pv(vt_ref, j0, p, per_head):
    ones = jnp.ones((SUM_ROWS, KEY_TILE), BF16)
    half = p.shape[1] // 2
    out = None
    for n in range(p.shape[0] // KEY_TILE):
        vt = vt_ref[j0 + n]
        pn = p[n * KEY_TILE:(n + 1) * KEY_TILE]
        if per_head:
            part = jnp.concatenate(
                [_dot(jnp.concatenate([vt[:HEAD_DIM], ones], axis=0), pn[:, :half]),
                 _dot(jnp.concatenate([vt[HEAD_DIM:], ones], axis=0), pn[:, half:])], axis=1)
        else:
            part = _dot(jnp.concatenate([vt, ones], axis=0), pn)
        out = part if out is None else out + part
    return out


def _softmax_step(s, s_max, vt_ref, j0, m_ref, acc_ref, per_head):
    m_prev = m_ref[...]
    m_new = jnp.maximum(m_prev, s_max)
    alpha = jnp.exp(m_prev - m_new)
    acc_ref[...] = alpha * acc_ref[...] + _pv(vt_ref, j0, jnp.exp(s - m_new).astype(BF16), per_head)
    m_ref[...] = m_new


def _softmax_init(m_ref, acc_ref):
    m_ref[...] = jnp.full_like(m_ref, MASKED)
    acc_ref[...] = jnp.zeros_like(acc_ref)


def _softmax_result(acc_ref, n_features):
    return acc_ref[:n_features] / acc_ref[n_features:n_features + 1]


def _col_max(s):
    return jnp.max(s, axis=0, keepdims=True)


def _pipelined_steps(n_regular, streams):
    def ahead(n, buf):
        for logits, _, _, s_ref, cmax_ref in streams:
            s = logits(n)
            s_ref[buf] = s
            cmax_ref[buf] = _col_max(s)

    def consume(n, buf):
        for _, update, _, s_ref, cmax_ref in streams:
            update(n, s_ref[buf], cmax_ref[buf])

    odd = n_regular % 2
    ahead(0, odd)

    @pl.when(odd == 1)
    def _():
        ahead(1, 0)
        consume(0, 1)

    def pair(nn, carry):
        t = odd + 2 * nn
        ahead(t + 1, 1)
        consume(t, 0)
        ahead(t + 2, 0)
        consume(t + 1, 1)
        return carry

    lax.fori_loop(0, n_regular // 2, pair, 0)
    for _, _, final, s_ref, _ in streams:
        final(s_ref[0])


def _step_buffers(tq):
    return [pltpu.VMEM((2, tq, 2 * tq), F32), pltpu.VMEM((2, 1, 2 * tq), F32)]


def _attn_call(body, name, qkv, extra_inputs, extra_specs, scratch, *, batch, seq, tq):
    rows_per_batch = seq // tq
    n_key_tiles = seq // KEY_TILE
    q_spec = pl.BlockSpec((tq, LANES), lambda b, h, i: (b * rows_per_batch + i, h))
    k_spec = pl.BlockSpec((seq, LANES), lambda b, h, i: (b, h))
    vt_spec = pl.BlockSpec((None, n_key_tiles, LANES, KEY_TILE), lambda b, h, i: (b, 0, h, 0))
    operands = []
    for q, k, vt in qkv:
        operands += [q, k, vt.reshape(batch, n_key_tiles, GROUP_W, KEY_TILE)]
    out_shape = jax.ShapeDtypeStruct(qkv[0][0].shape, BF16)
    outs = pl.pallas_call(
        body,
        grid=(batch, N_GROUPS, rows_per_batch),
        in_specs=extra_specs + [q_spec, k_spec, vt_spec] * len(qkv),
        out_specs=[q_spec] * len(qkv),
        out_shape=[out_shape] * len(qkv),
        scratch_shapes=scratch,
        compiler_params=_params(3),
        name=name,
    )(*extra_inputs, *operands)
    return outs if len(qkv) > 1 else outs[0]


def _row_stat(tq):
    return pltpu.VMEM((1, 2 * tq), F32)


def _acc(tq, n_features=LANES):
    return pltpu.VMEM((n_features, 2 * tq), F32)


def _diff_stream(q_ref, k_ref, vt_ref, m_ref, acc_ref, s_ref, cmax_ref, *, tq):
    i = pl.program_id(2)
    qq = _stack_heads(q_ref[...])
    _softmax_init(m_ref, acc_ref)
    tiles = tq // KEY_TILE

    def logits(n):
        k0 = pl.multiple_of(n * tq, tq)
        return _dot_nt(k_ref[pl.ds(k0, tq), :], qq)

    def update(n, s, s_max):
        _softmax_step(s, s_max, vt_ref, n * tiles, m_ref, acc_ref, False)

    def final(s):
        key, query = _tile_positions(tq, tq, 0, 0)
        s = jnp.where(key <= query, s, MASKED)
        _softmax_step(s, _col_max(s), vt_ref, i * tiles, m_ref, acc_ref, False)

    return logits, update, final, s_ref, cmax_ref


def _diff_finish(lam_ref, g_ref, o_ref, acc_ref, *, tq, post_scale):
    o_t = _softmax_result(acc_ref, LANES)
    out = (o_t[:, :tq] - lam_ref[0, 0] * o_t[:, tq:]).T
    o_ref[...] = (_rms(out, g_ref[...]) * post_scale).astype(BF16)


def _mixers_body(lam_ref, g_ref, km_ref, dq_ref, dk_ref, dvt_ref, sq_ref, sk_ref, svt_ref,
                 mq_ref, mk_ref, mvt_ref, a_ref, b_ref, c_ref,
                 dm_ref, dacc_ref, ds_ref, dcmax_ref, mm_ref, macc_ref, ms_ref, mcmax_ref,
                 sbc_ref, sbacc_ref, *, tq, post_scale):
    n_sub = tq // KEY_TILE
    sb = (sq_ref, sk_ref, svt_ref)
    diff = _diff_stream(dq_ref, dk_ref, dvt_ref, dm_ref, dacc_ref, ds_ref, dcmax_ref, tq=tq)
    moba = _moba_stream(km_ref, mq_ref, mk_ref, mvt_ref, mm_ref, macc_ref, ms_ref, mcmax_ref, tq=tq)
    _pipelined_steps(pl.program_id(2), [diff, moba])
    _diff_finish(lam_ref, g_ref, a_ref, dacc_ref, tq=tq, post_scale=post_scale)
    _moba_finish(c_ref, macc_ref, tq=tq)
    c_min = [_sb_start(h, *sb, sbc_ref, sbacc_ref, n_sub=n_sub) for h in range(n_sub)]
    for h in range(n_sub):
        _sb_finish(h, c_min[h], *sb, b_ref, sbc_ref, sbacc_ref, n_sub=n_sub)


def _mixers_attention(diff_qkv, sb_qkv, moba_qkv, lam, gain, kmean, *, batch, seq, post_scale,
                      tq=ATTN_TQ):
    body = functools.partial(_mixers_body, tq=tq, post_scale=post_scale)
    n_sub = tq // KEY_TILE
    return _attn_call(
        body, "mixers_attn", [diff_qkv, sb_qkv, moba_qkv],
        [lam, gain, kmean],
        [pl.BlockSpec(memory_space=pltpu.SMEM), pl.BlockSpec((1, LANES), lambda b, h, i: (0, 0)),
         pl.BlockSpec((1, kmean.shape[1], LANES), lambda b, h, i: (b, 0, h))],
        [_row_stat(tq), _acc(tq, LANES + SUM_ROWS)] + _step_buffers(tq)
        + [_row_stat(tq), _acc(tq, HEAD_DIM + SUM_ROWS)] + _step_buffers(tq)
        + [pltpu.VMEM((n_sub, 1, 2 * KEY_TILE), F32), pltpu.VMEM((n_sub, LANES, 2 * KEY_TILE), F32)],
        batch=batch, seq=seq, tq=tq)


def _sb_later():
    row = lax.broadcasted_iota(jnp.int32, (KEY_TILE + SUM_ROWS, 2 * KEY_TILE), 0)
    col = lax.broadcasted_iota(jnp.int32, row.shape, 1)
    col = jnp.where(col >= KEY_TILE, col - KEY_TILE, col)
    return jnp.logical_or(col > row, row >= KEY_TILE).astype(BF16)


def _sb_sub_tile(h, q_ref, n_sub):
    tq = KEY_TILE
    return pl.program_id(2) * n_sub + h, _stack_heads(q_ref[h * tq:(h + 1) * tq])


def _sb_start(h, q_ref, k_ref, vt_ref, c_ref, acc_ref, *, n_sub):
    i, qq = _sb_sub_tile(h, q_ref, n_sub)
    key, query = _tile_positions(KEY_TILE, KEY_TILE, 0, 0)
    return _sb_first_tiles(i, qq, k_ref, vt_ref, _sb_later(), key < query, c_ref.at[h], acc_ref.at[h])


def _sb_finish(h, c_min, q_ref, k_ref, vt_ref, o_ref, c_ref, acc_ref, *, n_sub):
    tq = KEY_TILE
    i, qq = _sb_sub_tile(h, q_ref, n_sub)
    _sb_earlier_tiles(i, qq, c_min, k_ref, vt_ref, _sb_later(), c_ref.at[h], acc_ref.at[h])
    o_ref[h * tq:(h + 1) * tq] = _unstack_heads(acc_ref[h], tq).astype(BF16)


def _sb_tile_terms(j, qq, k_ref, later, valid):
    tk = KEY_TILE
    k0 = pl.multiple_of(j * tk, tk)
    z = _dot_nt(k_ref[pl.ds(k0, tk), :], qq)
    softplus = jnp.maximum(z, 0.0) + jnp.log(1.0 + jnp.exp2(jnp.abs(z) * -LOG2_E))
    spent = softplus if valid is None else jnp.where(valid, softplus, 0.0)
    hi = spent.astype(BF16)
    lo = (spent - hi.astype(F32)).astype(BF16)
    sums = _dot(later, jnp.concatenate([hi, lo], axis=0))
    return (z - softplus) - sums[:tk], sums[tk:tk + 1]


def _sb_first_tiles(i, qq, k_ref, vt_ref, later, valid, c_ref, acc_ref):
    log_w, spent = _sb_tile_terms(i, qq, k_ref, later, valid)
    w = jnp.where(valid, jnp.exp(log_w), 0.0)
    has_prev = i > 0
    prev = jnp.maximum(i - 1, 0)
    log_w_prev, spent_prev = _sb_tile_terms(prev, qq, k_ref, later, None)
    w_prev = jnp.exp(log_w_prev - spent)
    vt_prev = vt_ref[prev]
    vt_prev = jnp.where(has_prev, vt_prev, jnp.zeros_like(vt_prev))
    acc_ref[...] = _dot(vt_ref[i], w.astype(BF16)) + _dot(vt_prev, w_prev.astype(BF16))
    c = spent + jnp.where(has_prev, spent_prev, 0.0)
    c_ref[...] = c
    return jnp.min(c)


def _sb_earlier_tiles(i, qq, c_min, k_ref, vt_ref, later, c_ref, acc_ref):
    def cond(state):
        j, c_min = state
        return jnp.logical_and(j >= 0, c_min < -SB_LOG_ZERO)

    def body(state):
        j, _ = state
        log_w, spent = _sb_tile_terms(j, qq, k_ref, later, None)
        c = c_ref[...]
        acc_ref[...] += _dot(vt_ref[j], jnp.exp(log_w - c).astype(BF16))
        c_ref[...] = c + spent
        return j - 1, jnp.min(c + spent)

    lax.while_loop(cond, body, (i - 2, c_min))


def _moba_stream(km_ref, q_ref, k_ref, vt_ref, m_ref, acc_ref, s_ref, cmax_ref, *, tq):
    i = pl.program_id(2)
    tk = MOBA_BLOCK
    blocks = tq // tk
    qq = _stack_heads(q_ref[...])

    gate = _dot_nt(km_ref[0], qq)
    blk = lax.broadcasted_iota(jnp.int32, gate.shape, 0)
    col = lax.broadcasted_iota(jnp.int32, (1, 2 * tq), 1)
    own = i * blocks + jnp.where(col >= tq, col - tq, col) // tk
    gate = jnp.where(blk < own, gate, -jnp.inf)
    selected = blk == own
    for _ in range(MOBA_TOPK):
        best = jnp.max(gate, axis=0, keepdims=True)
        idx = jnp.min(jnp.where(gate == best, blk, LANES), axis=0, keepdims=True)
        pick = jnp.logical_and(blk == idx, best > -jnp.inf)
        selected = jnp.logical_or(selected, pick)
        gate = jnp.where(blk == idx, -jnp.inf, gate)
    bias = jnp.where(selected, 0.0, MASKED)
    bias = jnp.concatenate([bias, jnp.zeros((LANES - bias.shape[0], 2 * tq), F32)], axis=0)
    qq = jnp.concatenate([qq, bias.T.astype(BF16)], axis=1)
    key_blk = lax.broadcasted_iota(jnp.int32, (tq, LANES), 0) // tk
    lane = lax.broadcasted_iota(jnp.int32, (tq, LANES), 1)

    _softmax_init(m_ref, acc_ref)

    def logits(n):
        k0 = pl.multiple_of(n * tq, tq)
        one_hot = (lane == key_blk + n * blocks).astype(BF16)
        return _dot_nt(jnp.concatenate([k_ref[pl.ds(k0, tq), :], one_hot], axis=1), qq)

    def update(n, s, s_max):
        _softmax_step(s, s_max, vt_ref, n * blocks, m_ref, acc_ref, True)

    def final(s):
        key, query = _tile_positions(tq, tq, 0, 0)
        s = jnp.where(key <= query, s, MASKED)
        _softmax_step(s, _col_max(s), vt_ref, i * blocks, m_ref, acc_ref, True)

    return logits, update, final, s_ref, cmax_ref


def _moba_finish(o_ref, acc_ref, *, tq):
    o_t = _softmax_result(acc_ref, HEAD_DIM)
    o_ref[...] = jnp.concatenate([o_t[:, :tq], o_t[:, tq:]], axis=0).T.astype(BF16)


def _merge_body(x_ref, a_ref, b_ref, c_ref, gate_ref, wa_ref, wb_ref, wc_ref, wo_ref, g_ref, o_ref):
    d = x_ref.shape[1]
    merged = (gate_ref[:, 0:d] * _dot(a_ref[...], wa_ref[...])
              + gate_ref[:, d:2 * d] * _dot(b_ref[...], wb_ref[...])
              + gate_ref[:, 2 * d:3 * d] * _dot(c_ref[...], wc_ref[...]))
    y = _dot(merged.astype(BF16), wo_ref[...])
    o_ref[...] = x_ref[...] + _rms(y, g_ref[...])


def _merge(x, a, b, c, gates, wa, wb, wc, wo, g, layer, *, tm=512):
    t, d = x.shape
    row = lambda w: pl.BlockSpec((tm, w), lambda i: (i, 0))
    weight = lambda w: _layer_spec(w, layer)
    return pl.pallas_call(
        _merge_body,
        grid=(t // tm,),
        in_specs=[row(d), row(GROUP_W), row(GROUP_W), row(GROUP_W), row(3 * d),
                  weight(wa), weight(wb), weight(wc), weight(wo),
                  pl.BlockSpec((1, d), lambda i: (0, 0))],
        out_specs=row(d),
        out_shape=jax.ShapeDtypeStruct((t, d), F32),
        compiler_params=_params(1),
        name="merge",
    )(x, a, b, c, gates, wa, wb, wc, wo, g)


def _rope_lane_tables(seq):
    pos = jnp.arange(seq, dtype=F32)
    inv_freq = ROPE_THETA ** (-jnp.arange(0, ROT_DIM, 2, dtype=F32) / ROT_DIM)
    ang = pos[:, None] * inv_freq[None, :]
    cos, sin = jnp.cos(ang), jnp.sin(ang)
    half = ROT_DIM // 2
    rest = HEAD_DIM - ROT_DIM
    cos_h = jnp.concatenate([cos, cos, jnp.ones((seq, rest), F32)], axis=1)
    sina_h = jnp.concatenate([-sin, jnp.zeros((seq, half + rest), F32)], axis=1)
    sinb_h = jnp.concatenate([jnp.zeros((seq, half), F32), sin, jnp.zeros((seq, rest), F32)], axis=1)
    reps = LANES // HEAD_DIM
    return tuple(jnp.tile(t, (1, reps)) for t in (cos_h, sina_h, sinb_h))


def kernel(x, w_in, w_diff_o, w_sb_o, w_moba_o, w_out, lam_q1, lam_k1, lam_q2, lam_k2,
           diff_norm_g, ffn1_wg, ffn1_wu, ffn1_wd, ffn2_wg, ffn2_wu, ffn2_wd,
           g_ffn1_pre, g_ffn1_post, g_mix_pre, g_mix_post, g_ffn2_pre, g_ffn2_post):
    batch, seq, d = x.shape
    depth = w_in.shape[0]
    n_blocks = seq // MOBA_BLOCK
    assert seq % 512 == 0 and n_blocks <= LANES
    cos_t, sina_t, sinb_t = _rope_lane_tables(seq)
    vec = lambda g: g.reshape(1, -1)
    xt = x.reshape(batch * seq, d)
    attn = dict(batch=batch, seq=seq)
    (w_in, w_diff_o, w_sb_o, w_moba_o, w_out, ffn1_wg, ffn1_wu, ffn1_wd, ffn2_wg, ffn2_wu, ffn2_wd) = (
        w.astype(BF16) for w in (w_in, w_diff_o, w_sb_o, w_moba_o, w_out,
                                 ffn1_wg, ffn1_wu, ffn1_wd, ffn2_wg, ffn2_wu, ffn2_wd))

    for l in range(depth):
        lambda_init = 0.8 - 0.6 * math.exp(-0.3 * l)
        xt = _ffn(xt, vec(g_ffn1_pre[l]), ffn1_wg, ffn1_wu, ffn1_wd, vec(g_ffn1_post[l]), l)

        (dq, dk, dvt, sq, sk, svt, mq, mk, mvt, gates, kmean) = _inproj(
            xt, vec(g_mix_pre[l]), w_in, l, cos_t, sina_t, sinb_t, seq)
        lam = (jnp.exp(jnp.sum(lam_q1[l] * lam_k1[l])) - jnp.exp(jnp.sum(lam_q2[l] * lam_k2[l]))
               + lambda_init).reshape(1, 1)
        kmean = kmean.reshape(batch, n_blocks, GROUP_W)
        kmean = jnp.pad(kmean, ((0, 0), (0, -n_blocks % BF16_ROWS), (0, 0))).astype(BF16)
        a_out, b_out, c_out = _mixers_attention(
            (dq, dk, dvt), (sq, sk, svt), (mq, mk, mvt), lam, vec(diff_norm_g[l]), kmean,
            post_scale=1.0 - lambda_init, **attn)

        xt = _merge(xt, a_out, b_out, c_out, gates, w_diff_o, w_sb_o, w_moba_o, w_out,
                    vec(g_mix_post[l]), l)

        xt = _ffn(xt, vec(g_ffn2_pre[l]), ffn2_wg, ffn2_wu, ffn2_wd, vec(g_ffn2_post[l]), l)
    return xt.reshape(batch, seq, d)
```

```python
import functools
import math

import jax
import jax.numpy as jnp
from jax import lax
from jax.experimental import pallas as pl
from jax.experimental.pallas import tpu as pltpu

F32 = jnp.float32
BF16 = jnp.bfloat16

LANES = 128
HEAD_DIM = 64
ROT_DIM = HEAD_DIM // 4
ROPE_THETA = 500000.0
MOBA_BLOCK = 256
MOBA_TOPK = 3
NORM_EPS = 1e-6
GROUP_W = 512
N_GROUPS = GROUP_W // LANES
N_QKV_GROUPS = 9
N_GATE_GROUPS = 6
V_GROUPS = (2, 5, 8)
KEY_TILE = MOBA_BLOCK
SB_LOG_ZERO = -104.0
LOG2_E = math.log2(math.e)
MASKED = -1e30
BF16_ROWS = 16
SUM_ROWS = BF16_ROWS
ATTN_TQ = 512
VMEM_LIMIT = 56 * 1024 * 1024

_NT = (((1,), (1,)), ((), ()))
_TN = (((0,), (1,)), ((), ()))


def _dot(a, b):
    return jnp.dot(a, b, preferred_element_type=F32)


def _dot_nt(a, b):
    return lax.dot_general(a, b, _NT, preferred_element_type=F32)


def _sigmoid(x):
    return 1.0 / (1.0 + jnp.exp(-x))


def _rms(x, gain):
    return x * lax.rsqrt(jnp.mean(x * x, axis=-1, keepdims=True) + NORM_EPS) * gain


def _params(n_axes):
    return pltpu.CompilerParams(dimension_semantics=("arbitrary",) * n_axes,
                                vmem_limit_bytes=VMEM_LIMIT)


def _ffn_body(x_ref, gpre_ref, wg_ref, wu_ref, wd_ref, gpost_ref, o_ref, *, tf):
    x = x_ref[...]
    xn = _rms(x, gpre_ref[...]).astype(BF16)
    y = None
    for c in range(wg_ref.shape[1] // tf):
        cols = slice(c * tf, (c + 1) * tf)
        g = _dot(xn, wg_ref[:, cols])
        u = _dot(xn, wu_ref[:, cols])
        part = _dot(((g * _sigmoid(g)) * u).astype(BF16), wd_ref[cols, :])
        y = part if y is None else y + part
    o_ref[...] = x + 0.5 * _rms(y, gpost_ref[...])


def _layer_spec(stacked, layer):
    return pl.BlockSpec((None,) + stacked.shape[1:], lambda i: (layer, 0, 0),
                        pipeline_mode=pl.Buffered(1))


def _ffn(x, gpre, wg, wu, wd, gpost, layer, *, tm=1024, tf=2816):
    t, d = x.shape
    return pl.pallas_call(
        functools.partial(_ffn_body, tf=tf),
        grid=(t // tm,),
        in_specs=[
            pl.BlockSpec((tm, d), lambda i: (i, 0)),
            pl.BlockSpec((1, d), lambda i: (0, 0)),
            _layer_spec(wg, layer), _layer_spec(wu, layer), _layer_spec(wd, layer),
            pl.BlockSpec((1, d), lambda i: (0, 0)),
        ],
        out_specs=pl.BlockSpec((tm, d), lambda i: (i, 0)),
        out_shape=jax.ShapeDtypeStruct((t, d), F32),
        compiler_params=_params(1),
        name="ffn",
    )(x, gpre, wg, wu, wd, gpost)


def _inproj_body(x_ref, g_ref, w_ref, c_ref, sa_ref, sb_ref,
                 dq_ref, dk_ref, dvt_ref, sq_ref, sk_ref, svt_ref, mq_ref, mk_ref, mvt_ref,
                 gate_ref, kmean_ref):
    xn = _rms(x_ref[...], g_ref[...]).astype(BF16)
    cos, sin_a, sin_b = c_ref[...], sa_ref[...], sb_ref[...]
    scale = HEAD_DIM ** -0.5

    def proj(group):
        return _dot(xn, w_ref[:, group * GROUP_W:(group + 1) * GROUP_W])

    def rope(t):
        half = ROT_DIM // 2
        return (t * cos + pltpu.roll(t, LANES - half, 1) * sin_a
                + pltpu.roll(t, half, 1) * sin_b)

    def emit(out_ref, group, roped, scaled):
        t = proj(group)
        slabs = []
        for s in range(N_GROUPS):
            ts = t[:, s * LANES:(s + 1) * LANES]
            if roped:
                ts = rope(ts)
            slabs.append(ts)
            out_ref[:, s * LANES:(s + 1) * LANES] = (ts * scale if scaled else ts).astype(BF16)
        return slabs

    emit(dq_ref, 0, True, True)
    emit(dk_ref, 1, True, False)
    emit(sq_ref, 3, False, True)
    emit(sk_ref, 4, False, False)
    emit(mq_ref, 6, True, True)
    mk_slabs = emit(mk_ref, 7, True, False)
    for s, ts in enumerate(mk_slabs):
        kmean_ref[0, :, s * LANES:(s + 1) * LANES] = jnp.mean(ts, axis=0, keepdims=True)
    for group, vt_ref in zip(V_GROUPS, (dvt_ref, svt_ref, mvt_ref)):
        vt_ref[0] = lax.dot_general(w_ref[:, group * GROUP_W:(group + 1) * GROUP_W], xn, _TN,
                                    preferred_element_type=F32).astype(BF16)
    for gi in range(N_GATE_GROUPS):
        gate_ref[:, gi * GROUP_W:(gi + 1) * GROUP_W] = _sigmoid(proj(N_QKV_GROUPS + gi)).astype(BF16)


def _inproj(x, g, w, layer, cos_t, sina_t, sinb_t, seq):
    t, d = x.shape
    tm = KEY_TILE
    n_pos_tiles = seq // tm
    qk_spec = pl.BlockSpec((tm, GROUP_W), lambda i: (i, 0))
    vt_spec = pl.BlockSpec((1, GROUP_W, tm), lambda i: (i, 0, 0))
    tab_spec = pl.BlockSpec((tm, LANES), lambda i: (i % n_pos_tiles, 0))
    qk_shape = jax.ShapeDtypeStruct((t, GROUP_W), BF16)
    vt_shape = jax.ShapeDtypeStruct((t // tm, GROUP_W, tm), BF16)
    gate_w = N_GATE_GROUPS * GROUP_W
    return pl.pallas_call(
        _inproj_body,
        grid=(t // tm,),
        in_specs=[
            pl.BlockSpec((tm, d), lambda i: (i, 0)),
            pl.BlockSpec((1, d), lambda i: (0, 0)),
            _layer_spec(w, layer),
            tab_spec, tab_spec, tab_spec,
        ],
        out_specs=[qk_spec, qk_spec, vt_spec] * 3 + [
            pl.BlockSpec((tm, gate_w), lambda i: (i, 0)),
            pl.BlockSpec((1, 1, GROUP_W), lambda i: (i, 0, 0)),
        ],
        out_shape=[qk_shape, qk_shape, vt_shape] * 3 + [
            jax.ShapeDtypeStruct((t, gate_w), BF16),
            jax.ShapeDtypeStruct((t // tm, 1, GROUP_W), F32),
        ],
        compiler_params=_params(1),
        name="inproj",
    )(x, g, w, cos_t, sina_t, sinb_t)


def _stack_heads(q):
    lane = lax.broadcasted_iota(jnp.int32, q.shape, 1)
    zero = jnp.zeros_like(q)
    return jnp.concatenate([jnp.where(lane < HEAD_DIM, q, zero),
                            jnp.where(lane >= HEAD_DIM, q, zero)], axis=0)


def _unstack_heads(o_t, tq):
    return jnp.concatenate([o_t[:HEAD_DIM, :tq], o_t[HEAD_DIM:, tq:]], axis=0).T


def _tile_positions(tk, tq, k0, q0):
    key = lax.broadcasted_iota(jnp.int32, (tk, 2 * tq), 0) + k0
    col = lax.broadcasted_iota(jnp.int32, (tk, 2 * tq), 1)
    query = jnp.where(col >= tq, col - tq, col) + q0
    return key, query


def _pv(vt_ref, j0, p, per_head):
    ones = jnp.ones((SUM_ROWS, KEY_TILE), BF16)
    half = p.shape[1] // 2
    out = None
    for n in range(p.shape[0] // KEY_TILE):
        vt = vt_ref[j0 + n]
        pn = p[n * KEY_TILE:(n + 1) * KEY_TILE]
        if per_head:
            part = jnp.concatenate(
                [_dot(jnp.concatenate([vt[:HEAD_DIM], ones], axis=0), pn[:, :half]),
                 _dot(jnp.concatenate([vt[HEAD_DIM:], ones], axis=0), pn[:, half:])], axis=1)
        else:
            part = _dot(jnp.concatenate([vt, ones], axis=0), pn)
        out = part if out is None else out + part
    return out


def _softmax_step(s, s_max, vt_ref, j0, m_ref, acc_ref, per_head):
    m_prev = m_ref[...]
    m_new = jnp.maximum(m_prev, s_max)
    alpha = jnp.exp(m_prev - m_new)
    acc_ref[...] = alpha * acc_ref[...] + _pv(vt_ref, j0, jnp.exp(s - m_new).astype(BF16), per_head)
    m_ref[...] = m_new


def _softmax_init(m_ref, acc_ref):
    m_ref[...] = jnp.full_like(m_ref, MASKED)
    acc_ref[...] = jnp.zeros_like(acc_ref)


def _softmax_result(acc_ref, n_features):
    return acc_ref[:n_features] / acc_ref[n_features:n_features + 1]


def _col_max(s):
    return jnp.max(s, axis=0, keepdims=True)


def _pipelined_steps(n_regular, streams):
    def ahead(n, buf):
        for logits, _, _, s_ref, cmax_ref in streams:
            s = logits(n)
            s_ref[buf] = s
            cmax_ref[buf] = _col_max(s)

    def consume(n, buf):
        for _, update, _, s_ref, cmax_ref in streams:
            update(n, s_ref[buf], cmax_ref[buf])

    odd = n_regular % 2
    ahead(0, odd)

    @pl.when(odd == 1)
    def _():
        ahead(1, 0)
        consume(0, 1)

    def pair(nn, carry):
        t = odd + 2 * nn
        ahead(t + 1, 1)
        consume(t, 0)
        ahead(t + 2, 0)
        consume(t + 1, 1)
        return carry

    lax.fori_loop(0, n_regular // 2, pair, 0)
    for _, _, final, s_ref, _ in streams:
        final(s_ref[0])


def _step_buffers(tq):
    return [pltpu.VMEM((2, tq, 2 * tq), F32), pltpu.VMEM((2, 1, 2 * tq), F32)]


def _attn_call(body, name, qkv, extra_inputs, extra_specs, scratch, *, batch, seq, tq):
    rows_per_batch = seq // tq
    n_key_tiles = seq // KEY_TILE
    q_spec = pl.BlockSpec((tq, LANES), lambda b, h, i: (b * rows_per_batch + i, h))
    k_spec = pl.BlockSpec((seq, LANES), lambda b, h, i: (b, h))
    vt_spec = pl.BlockSpec((None, n_key_tiles, LANES, KEY_TILE), lambda b, h, i: (b, 0, h, 0))
    operands = []
    for q, k, vt in qkv:
        operands += [q, k, vt.reshape(batch, n_key_tiles, GROUP_W, KEY_TILE)]
    out_shape = jax.ShapeDtypeStruct(qkv[0][0].shape, BF16)
    outs = pl.pallas_call(
        body,
        grid=(batch, N_GROUPS, rows_per_batch),
        in_specs=extra_specs + [q_spec, k_spec, vt_spec] * len(qkv),
        out_specs=[q_spec] * len(qkv),
        out_shape=[out_shape] * len(qkv),
        scratch_shapes=scratch,
        compiler_params=_params(3),
        name=name,
    )(*extra_inputs, *operands)
    return outs if len(qkv) > 1 else outs[0]


def _row_stat(tq):
    return pltpu.VMEM((1, 2 * tq), F32)


def _acc(tq, n_features=LANES):
    return pltpu.VMEM((n_features, 2 * tq), F32)


def _diff_stream(q_ref, k_ref, vt_ref, m_ref, acc_ref, s_ref, cmax_ref, *, tq):
    i = pl.program_id(2)
    qq = _stack_heads(q_ref[...])
    _softmax_init(m_ref, acc_ref)
    tiles = tq // KEY_TILE

    def logits(n):
        k0 = pl.multiple_of(n * tq, tq)
        return _dot_nt(k_ref[pl.ds(k0, tq), :], qq)

    def update(n, s, s_max):
        _softmax_step(s, s_max, vt_ref, n * tiles, m_ref, acc_ref, False)

    def final(s):
        key, query = _tile_positions(tq, tq, 0, 0)
        s = jnp.where(key <= query, s, MASKED)
        _softmax_step(s, _col_max(s), vt_ref, i * tiles, m_ref, acc_ref, False)

    return logits, update, final, s_ref, cmax_ref


def _diff_finish(lam_ref, g_ref, o_ref, acc_ref, *, tq, post_scale):
    o_t = _softmax_result(acc_ref, LANES)
    out = (o_t[:, :tq] - lam_ref[0, 0] * o_t[:, tq:]).T
    o_ref[...] = (_rms(out, g_ref[...]) * post_scale).astype(BF16)


def _mixers_body(lam_ref, g_ref, km_ref, dq_ref, dk_ref, dvt_ref, sq_ref, sk_ref, svt_ref,
                 mq_ref, mk_ref, mvt_ref, a_ref, b_ref, c_ref,
                 dm_ref, dacc_ref, ds_ref, dcmax_ref, mm_ref, macc_ref, ms_ref, mcmax_ref,
                 sbc_ref, sbacc_ref, *, tq, post_scale):
    n_sub = tq // KEY_TILE
    sb = (sq_ref, sk_ref, svt_ref)
    diff = _diff_stream(dq_ref, dk_ref, dvt_ref, dm_ref, dacc_ref, ds_ref, dcmax_ref, tq=tq)
    moba = _moba_stream(km_ref, mq_ref, mk_ref, mvt_ref, mm_ref, macc_ref, ms_ref, mcmax_ref, tq=tq)
    _pipelined_steps(pl.program_id(2), [diff, moba])
    _diff_finish(lam_ref, g_ref, a_ref, dacc_ref, tq=tq, post_scale=post_scale)
    _moba_finish(c_ref, macc_ref, tq=tq)
    c_min = [_sb_start(h, *sb, sbc_ref, sbacc_ref, n_sub=n_sub) for h in range(n_sub)]
    for h in range(n_sub):
        _sb_finish(h, c_min[h], *sb, b_ref, sbc_ref, sbacc_ref, n_sub=n_sub)


def _mixers_attention(diff_qkv, sb_qkv, moba_qkv, lam, gain, kmean, *, batch, seq, post_scale,
                      tq=ATTN_TQ):
    body = functools.partial(_mixers_body, tq=tq, post_scale=post_scale)
    n_sub = tq // KEY_TILE
    return _attn_call(
        body, "mixers_attn", [diff_qkv, sb_qkv, moba_qkv],
        [lam, gain, kmean],
        [pl.BlockSpec(memory_space=pltpu.SMEM), pl.BlockSpec((1, LANES), lambda b, h, i: (0, 0)),
         pl.BlockSpec((1, kmean.shape[1], LANES), lambda b, h, i: (b, 0, h))],
        [_row_stat(tq), _acc(tq, LANES + SUM_ROWS)] + _step_buffers(tq)
        + [_row_stat(tq), _acc(tq, HEAD_DIM + SUM_ROWS)] + _step_buffers(tq)
        + [pltpu.VMEM((n_sub, 1, 2 * KEY_TILE), F32), pltpu.VMEM((n_sub, LANES, 2 * KEY_TILE), F32)],
        batch=batch, seq=seq, tq=tq)


def _sb_later():
    row = lax.broadcasted_iota(jnp.int32, (KEY_TILE + SUM_ROWS, 2 * KEY_TILE), 0)
    col = lax.broadcasted_iota(jnp.int32, row.shape, 1)
    col = jnp.where(col >= KEY_TILE, col - KEY_TILE, col)
    return jnp.logical_or(col > row, row >= KEY_TILE).astype(BF16)


def _sb_sub_tile(h, q_ref, n_sub):
    tq = KEY_TILE
    return pl.program_id(2) * n_sub + h, _stack_heads(q_ref[h * tq:(h + 1) * tq])


def _sb_start(h, q_ref, k_ref, vt_ref, c_ref, acc_ref, *, n_sub):
    i, qq = _sb_sub_tile(h, q_ref, n_sub)
    key, query = _tile_positions(KEY_TILE, KEY_TILE, 0, 0)
    return _sb_first_tiles(i, qq, k_ref, vt_ref, _sb_later(), key < query, c_ref.at[h], acc_ref.at[h])


def _sb_finish(h, c_min, q_ref, k_ref, vt_ref, o_ref, c_ref, acc_ref, *, n_sub):
    tq = KEY_TILE
    i, qq = _sb_sub_tile(h, q_ref, n_sub)
    _sb_earlier_tiles(i, qq, c_min, k_ref, vt_ref, _sb_later(), c_ref.at[h], acc_ref.at[h])
    o_ref[h * tq:(h + 1) * tq] = _unstack_heads(acc_ref[h], tq).astype(BF16)


def _sb_tile_terms(j, qq, k_ref, later, valid):
    tk = KEY_TILE
    k0 = pl.multiple_of(j * tk, tk)
    z = _dot_nt(k_ref[pl.ds(k0, tk), :], qq)
    softplus = jnp.maximum(z, 0.0) + jnp.log(1.0 + jnp.exp2(jnp.abs(z) * -LOG2_E))
    spent = softplus if valid is None else jnp.where(valid, softplus, 0.0)
    hi = spent.astype(BF16)
    lo = (spent - hi.astype(F32)).astype(BF16)
    sums = _dot(later, jnp.concatenate([hi, lo], axis=0))
    return (z - softplus) - sums[:tk], sums[tk:tk + 1]


def _sb_first_tiles(i, qq, k_ref, vt_ref, later, valid, c_ref, acc_ref):
    log_w, spent = _sb_tile_terms(i, qq, k_ref, later, valid)
    w = jnp.where(valid, jnp.exp(log_w), 0.0)
    has_prev = i > 0
    prev = jnp.maximum(i - 1, 0)
    log_w_prev, spent_prev = _sb_tile_terms(prev, qq, k_ref, later, None)
    w_prev = jnp.exp(log_w_prev - spent)
    vt_prev = vt_ref[prev]
    vt_prev = jnp.where(has_prev, vt_prev, jnp.zeros_like(vt_prev))
    acc_ref[...] = _dot(vt_ref[i], w.astype(BF16)) + _dot(vt_prev, w_prev.astype(BF16))
    c = spent + jnp.where(has_prev, spent_prev, 0.0)
    c_ref[...] = c
    return jnp.min(c)


def _sb_earlier_tiles(i, qq, c_min, k_ref, vt_ref, later, c_ref, acc_ref):
    def cond(state):
        j, c_min = state
        return jnp.logical_and(j >= 0, c_min < -SB_LOG_ZERO)

    def body(state):
        j, _ = state
        log_w, spent = _sb_tile_terms(j, qq, k_ref, later, None)
        c = c_ref[...]
        acc_ref[...] += _dot(vt_ref[j], jnp.exp(log_w - c).astype(BF16))
        c_ref[...] = c + spent
        return j - 1, jnp.min(c + spent)

    lax.while_loop(cond, body, (i - 2, c_min))


def _moba_stream(km_ref, q_ref, k_ref, vt_ref, m_ref, acc_ref, s_ref, cmax_ref, *, tq):
    i = pl.program_id(2)
    tk = MOBA_BLOCK
    blocks = tq // tk
    qq = _stack_heads(q_ref[...])

    gate = _dot_nt(km_ref[0], qq)
    blk = lax.broadcasted_iota(jnp.int32, gate.shape, 0)
    col = lax.broadcasted_iota(jnp.int32, (1, 2 * tq), 1)
    own = i * blocks + jnp.where(col >= tq, col - tq, col) // tk
    gate = jnp.where(blk < own, gate, -jnp.inf)
    selected = blk == own
    for _ in range(MOBA_TOPK):
        best = jnp.max(gate, axis=0, keepdims=True)
        idx = jnp.min(jnp.where(gate == best, blk, LANES), axis=0, keepdims=True)
        pick = jnp.logical_and(blk == idx, best > -jnp.inf)
        selected = jnp.logical_or(selected, pick)
        gate = jnp.where(blk == idx, -jnp.inf, gate)
    bias = jnp.where(selected, 0.0, MASKED)
    bias = jnp.concatenate([bias, jnp.zeros((LANES - bias.shape[0], 2 * tq), F32)], axis=0)
    qq = jnp.concatenate([qq, bias.T.astype(BF16)], axis=1)
    key_blk = lax.broadcasted_iota(jnp.int32, (tq, LANES), 0) // tk
    lane = lax.broadcasted_iota(jnp.int32, (tq, LANES), 1)

    _softmax_init(m_ref, acc_ref)

    def logits(n):
        k0 = pl.multiple_of(n * tq, tq)
        one_hot = (lane == key_blk + n * blocks).astype(BF16)
        return _dot_nt(jnp.concatenate([k_ref[pl.ds(k0, tq), :], one_hot], axis=1), qq)

    def update(n, s, s_max):
        _softmax_step(s, s_max, vt_ref, n * blocks, m_ref, acc_ref, True)

    def final(s):
        key, query = _tile_positions(tq, tq, 0, 0)
        s = jnp.where(key <= query, s, MASKED)
        _softmax_step(s, _col_max(s), vt_ref, i * blocks, m_ref, acc_ref, True)

    return logits, update, final, s_ref, cmax_ref


def _moba_finish(o_ref, acc_ref, *, tq):
    o_t = _softmax_result(acc_ref, HEAD_DIM)
    o_ref[...] = jnp.concatenate([o_t[:, :tq], o_t[:, tq:]], axis=0).T.astype(BF16)


def _merge_body(x_ref, a_ref, b_ref, c_ref, gate_ref, wa_ref, wb_ref, wc_ref, wo_ref, g_ref, o_ref):
    d = x_ref.shape[1]
    merged = (gate_ref[:, 0:d] * _dot(a_ref[...], wa_ref[...])
              + gate_ref[:, d:2 * d] * _dot(b_ref[...], wb_ref[...])
              + gate_ref[:, 2 * d:3 * d] * _dot(c_ref[...], wc_ref[...]))
    y = _dot(merged.astype(BF16), wo_ref[...])
    o_ref[...] = x_ref[...] + _rms(y, g_ref[...])


def _merge(x, a, b, c, gates, wa, wb, wc, wo, g, layer, *, tm=512):
    t, d = x.shape
    row = lambda w: pl.BlockSpec((tm, w), lambda i: (i, 0))
    weight = lambda w: _layer_spec(w, layer)
    return pl.pallas_call(
        _merge_body,
        grid=(t // tm,),
        in_specs=[row(d), row(GROUP_W), row(GROUP_W), row(GROUP_W), row(3 * d),
                  weight(wa), weight(wb), weight(wc), weight(wo),
                  pl.BlockSpec((1, d), lambda i: (0, 0))],
        out_specs=row(d),
        out_shape=jax.ShapeDtypeStruct((t, d), F32),
        compiler_params=_params(1),
        name="merge",
    )(x, a, b, c, gates, wa, wb, wc, wo, g)


def _rope_lane_tables(seq):
    pos = jnp.arange(seq, dtype=F32)
    inv_freq = ROPE_THETA ** (-jnp.arange(0, ROT_DIM, 2, dtype=F32) / ROT_DIM)
    ang = pos[:, None] * inv_freq[None, :]
    cos, sin = jnp.cos(ang), jnp.sin(ang)
    half = ROT_DIM // 2
    rest = HEAD_DIM - ROT_DIM
    cos_h = jnp.concatenate([cos, cos, jnp.ones((seq, rest), F32)], axis=1)
    sina_h = jnp.concatenate([-sin, jnp.zeros((seq, half + rest), F32)], axis=1)
    sinb_h = jnp.concatenate([jnp.zeros((seq, half), F32), sin, jnp.zeros((seq, rest), F32)], axis=1)
    reps = LANES // HEAD_DIM
    return tuple(jnp.tile(t, (1, reps)) for t in (cos_h, sina_h, sinb_h))


def kernel(x, w_in, w_diff_o, w_sb_o, w_moba_o, w_out, lam_q1, lam_k1, lam_q2, lam_k2,
           diff_norm_g, ffn1_wg, ffn1_wu, ffn1_wd, ffn2_wg, ffn2_wu, ffn2_wd,
           g_ffn1_pre, g_ffn1_post, g_mix_pre, g_mix_post, g_ffn2_pre, g_ffn2_post):
    batch, seq, d = x.shape
    depth = w_in.shape[0]
    n_blocks = seq // MOBA_BLOCK
    assert seq % 512 == 0 and n_blocks <= LANES
    cos_t, sina_t, sinb_t = _rope_lane_tables(seq)
    vec = lambda g: g.reshape(1, -1)
    xt = x.reshape(batch * seq, d)
    attn = dict(batch=batch, seq=seq)
    (w_in, w_diff_o, w_sb_o, w_moba_o, w_out, ffn1_wg, ffn1_wu, ffn1_wd, ffn2_wg, ffn2_wu, ffn2_wd) = (
        w.astype(BF16) for w in (w_in, w_diff_o, w_sb_o, w_moba_o, w_out,
                                 ffn1_wg, ffn1_wu, ffn1_wd, ffn2_wg, ffn2_wu, ffn2_wd))

    for l in range(depth):
        lambda_init = 0.8 - 0.6 * math.exp(-0.3 * l)
        xt = _ffn(xt, vec(g_ffn1_pre[l]), ffn1_wg, ffn1_wu, ffn1_wd, vec(g_ffn1_post[l]), l)

        (dq, dk, dvt, sq, sk, svt, mq, mk, mvt, gates, kmean) = _inproj(
            xt, vec(g_mix_pre[l]), w_in, l, cos_t, sina_t, sinb_t, seq)
        lam = (jnp.exp(jnp.sum(lam_q1[l] * lam_k1[l])) - jnp.exp(jnp.sum(lam_q2[l] * lam_k2[l]))
               + lambda_init).reshape(1, 1)
        kmean = kmean.reshape(batch, n_blocks, GROUP_W)
        kmean = jnp.pad(kmean, ((0, 0), (0, -n_blocks % BF16_ROWS), (0, 0))).astype(BF16)
        a_out, b_out, c_out = _mixers_attention(
            (dq, dk, dvt), (sq, sk, svt), (mq, mk, mvt), lam, vec(diff_norm_g[l]), kmean,
            post_scale=1.0 - lambda_init, **attn)

        xt = _merge(xt, a_out, b_out, c_out, gates, w_diff_o, w_sb_o, w_moba_o, w_out,
                    vec(g_mix_post[l]), l)

        xt = _ffn(xt, vec(g_ffn2_pre[l]), ffn2_wg, ffn2_wu, ffn2_wd, vec(g_ffn2_post[l]), l)
    return xt.reshape(batch, seq, d)
```
